```python
import math
import jax, jax.numpy as jnp
from jax import lax
import numpy as np

D_MODEL = 1024
BATCH = 4
SEQ = 4096
DEPTH = 2
DEC_BATCH = 32
DEC_SEQ = 8
PAST_LEN = 8192
PAGE_SIZE = 128

N_EVEN = (DEPTH + 1) // 2
N_ODD = DEPTH // 2
EPS = 1e-6
F32 = jnp.float32

POOL_WINDOWS = (2, 4, 8, 16)
N_POOL_GROUPS = len(POOL_WINDOWS)
POOL_GROUP_DIM = D_MODEL // 8
D_POOL = N_POOL_GROUPS * POOL_GROUP_DIM
POOL_HIST = max(POOL_WINDOWS) - 1

DN_HEADS = 4
DN_HEAD_DIM = 128
D_DN = DN_HEADS * DN_HEAD_DIM
DN_CONV = 4
DN_CHUNK = 64

OFF_Q = D_POOL
OFF_GATE = D_POOL + 3 * D_DN
OFF_BETA = D_POOL + 4 * D_DN
OFF_ALPHA = OFF_BETA + DN_HEADS
D_IN_EVEN = OFF_ALPHA + DN_HEADS
D_MIX_EVEN = D_POOL + D_DN

WIN_CONFIGS = ((128, 1), (512, 4), (2048, 16))
N_WIN = len(WIN_CONFIGS)
SWA_HEADS = 8
SWA_HEAD_DIM = 64
D_SWA = SWA_HEADS * SWA_HEAD_DIM
ROT_DIM = SWA_HEAD_DIM // 4
ROPE_THETA = 500000.0
SWA_BLOCK = 128

N_MEM = 256
MEM_HEADS = 4
MEM_HEAD_DIM = D_MODEL // MEM_HEADS

D_FF = 2816
FFN_CONV = 3

kernel_name = 'hybrid_pool_delta_dilated_decoder_step'


def rmsnorm(x, g):
    xf = x.astype(F32)
    y = xf * lax.rsqrt(jnp.mean(xf * xf, axis=-1, keepdims=True) + EPS)
    return (y * g.astype(F32)).astype(x.dtype)


def l2norm(x):
    return x * lax.rsqrt(jnp.sum(x * x, axis=-1, keepdims=True) + EPS)


def causal_dwconv(x_ext, w):
    return lax.conv_general_dilated(x_ext, w[:, None, :].astype(x_ext.dtype), (1,), 'VALID',
                                    dimension_numbers=('NWC', 'WIO', 'NWC'),
                                    feature_group_count=x_ext.shape[-1])


def partial_rope(x, pos):
    half = ROT_DIM // 2
    inv_freq = ROPE_THETA ** (-jnp.arange(half, dtype=F32) / half)
    ang = pos.astype(F32)[:, None] * inv_freq[None, :]
    shape = (1, pos.shape[0]) + (1,) * (x.ndim - 3) + (half,)
    cos, sin = jnp.cos(ang).reshape(shape), jnp.sin(ang).reshape(shape)
    xr = x[..., :ROT_DIM].astype(F32)
    x1, x2 = xr[..., :half], xr[..., half:]
    rot = jnp.concatenate([x1 * cos - x2 * sin, x2 * cos + x1 * sin], axis=-1).astype(x.dtype)
    return jnp.concatenate([rot, x[..., ROT_DIM:]], axis=-1)


def pool_mix(a_ext, n_hist, p0, w_pool, pool_scale):
    B, L, _ = a_ext.shape
    T = L - n_hist
    af = a_ext.astype(F32)
    cs = jnp.pad(jnp.cumsum(af, axis=1), ((0, 0), (1, 0), (0, 0)))
    idx = n_hist + jnp.arange(T)
    pos = p0 + jnp.arange(T)
    cur = af[:, n_hist:]
    groups = []
    for gi, w in enumerate(POOL_WINDOWS):
        sl = slice(gi * POOL_GROUP_DIM, (gi + 1) * POOL_GROUP_DIM)
        lo = jnp.maximum(idx + 1 - w, 0)
        cnt = jnp.minimum(pos + 1, w).astype(F32)[None, :, None]
        groups.append((cs[:, idx + 1, sl] - cs[:, lo, sl]) / cnt - cur[:, :, sl])
    z = jnp.stack(groups, axis=2)
    y = jnp.einsum('btgc,gcd->btgd', z, w_pool.astype(F32))
    y = y * pool_scale.astype(F32).reshape(N_POOL_GROUPS, POOL_GROUP_DIM)
    return y.reshape(B, T, D_POOL).astype(a_ext.dtype)


def gated_delta_rule(q, k, v, g, beta, s0):
    B, T, H, K = q.shape
    V = v.shape[-1]
    C = math.gcd(T, DN_CHUNK)
    n = T // C

    def blk(t):
        t = t.reshape((B, n, C) + t.shape[2:])
        return jnp.moveaxis(jnp.moveaxis(t, 1, 0), 3, 2)

    qc, kc, vc, gc, bc = blk(q), blk(k), blk(v), blk(g), blk(beta)
    G = jnp.cumsum(gc, axis=-1)
    i = jnp.arange(C)
    incl = i[:, None] >= i[None, :]
    strict = i[:, None] > i[None, :]
    decay = jnp.exp(jnp.where(incl, G[..., :, None] - G[..., None, :], -jnp.inf))
    a = jnp.where(strict, jnp.einsum('nbhik,nbhjk->nbhij', kc, kc) * decay * bc[..., :, None], 0.0)
    rhs = jnp.concatenate([vc * bc[..., None], kc * (bc * jnp.exp(G))[..., None]], axis=-1)
    sol = lax.linalg.triangular_solve(a, rhs, left_side=True, lower=True, unit_diagonal=True)
    u, w = sol[..., :V], sol[..., V:]
    qk = jnp.einsum('nbhik,nbhjk->nbhij', qc, kc) * decay
    qg = qc * jnp.exp(G)[..., None]
    kg = kc * jnp.exp(G[..., -1:] - G)[..., None]
    g_last = jnp.exp(G[..., -1])

    def step(s, xs):
        u_c, w_c, qk_c, qg_c, kg_c, gl_c = xs
        v_new = u_c - jnp.einsum('bhck,bhkv->bhcv', w_c, s)
        o = jnp.einsum('bhck,bhkv->bhcv', qg_c, s) + jnp.einsum('bhij,bhjv->bhiv', qk_c, v_new)
        s = s * gl_c[..., None, None] + jnp.einsum('bhck,bhcv->bhkv', kg_c, v_new)
        return s, o

    s_T, o = lax.scan(step, s0, (u, w, qk, qg, kg, g_last))
    o = o.transpose(1, 0, 3, 2, 4).reshape(B, T, H, V)
    return o, s_T


def even_mixer(h, pool_hist, conv_hist, s0, p0, w_in, w_pool, pool_scale, conv_w, a_log, dt_bias,
               norm_w, w_out):
    B, T, _ = h.shape
    proj = h @ w_in
    a_in = proj[..., :OFF_Q]
    qkv = proj[..., OFF_Q:OFF_GATE]
    gate = proj[..., OFF_GATE:OFF_BETA].astype(F32).reshape(B, T, DN_HEADS, DN_HEAD_DIM)
    b_raw = proj[..., OFF_BETA:OFF_ALPHA].astype(F32)
    a_raw = proj[..., OFF_ALPHA:].astype(F32)
    a_ext = jnp.concatenate([pool_hist, a_in], axis=1)
    y_pool = pool_mix(a_ext, pool_hist.shape[1], p0, w_pool, pool_scale)
    qkv_ext = jnp.concatenate([conv_hist, qkv], axis=1)
    qkv_c = jax.nn.silu(causal_dwconv(qkv_ext, conv_w).astype(F32)).reshape(B, T, 3, DN_HEADS, DN_HEAD_DIM)
    q = l2norm(qkv_c[:, :, 0]) * DN_HEAD_DIM ** -0.5
    k = l2norm(qkv_c[:, :, 1])
    v = qkv_c[:, :, 2]
    beta = jax.nn.sigmoid(b_raw)
    g = -jnp.exp(a_log.astype(F32)) * jax.nn.softplus(a_raw + dt_bias.astype(F32))
    o, s_new = gated_delta_rule(q, k, v, g, beta, s0.astype(F32))
    o = o * lax.rsqrt(jnp.mean(o * o, axis=-1, keepdims=True) + EPS) * norm_w.astype(F32) * jax.nn.silu(gate)
    mix = jnp.concatenate([y_pool, o.reshape(B, T, D_DN).astype(h.dtype)], axis=-1)
    return mix @ w_out, a_ext[:, -POOL_HIST:], qkv_ext[:, -(DN_CONV - 1):], s_new


def swa_project(h, pos, w_qkv):
    B, T, _ = h.shape
    qkv = (h @ w_qkv).reshape(B, T, N_WIN, 3, SWA_HEADS, SWA_HEAD_DIM)
    return partial_rope(qkv[:, :, :, 0], pos), partial_rope(qkv[:, :, :, 1], pos), qkv[:, :, :, 2]


def dilated_group_prompt(q, k, v, window, dilation):
    B, S, H, E = q.shape
    span = window // dilation
    L = S // dilation
    nb = -(-L // SWA_BLOCK)
    Lp = nb * SWA_BLOCK

    def sub(t):
        t = t.reshape(B, L, dilation, H, E).transpose(0, 2, 1, 3, 4)
        return jnp.pad(t, ((0, 0), (0, 0), (0, Lp - L), (0, 0), (0, 0)))

    def band(t):
        t = jnp.pad(sub(t), ((0, 0), (0, 0), (SWA_BLOCK, 0), (0, 0), (0, 0)))
        t = t.reshape(B, dilation, nb + 1, SWA_BLOCK, H, E)
        return jnp.concatenate([t[:, :, :-1], t[:, :, 1:]], axis=3)

    qb = sub(q).reshape(B, dilation, nb, SWA_BLOCK, H, E)
    kb, vb = band(k), band(v)
    s = jnp.einsum('bdnqhe,bdnkhe->bdnhqk', qb, kb, preferred_element_type=F32) * E ** -0.5
    blk_i = jnp.arange(nb)[:, None, None]
    qi = jnp.arange(SWA_BLOCK)[None, :, None]
    kj = jnp.arange(2 * SWA_BLOCK)[None, None, :]
    dist = SWA_BLOCK + qi - kj
    valid = (dist >= 0) & (dist <= span) & ((blk_i - 1) * SWA_BLOCK + kj >= 0)
    s = jnp.where(valid[None, None, :, None], s, -jnp.inf)
    m = jnp.max(s, axis=-1)
    p = jnp.exp(s - m[..., None])
    den = jnp.sum(p, axis=-1)
    num = jnp.einsum('bdnhqk,bdnkhe->bdnqhe', p, vb.astype(F32))

    def unsub(t):
        t = t.reshape((B, dilation, Lp) + t.shape[4:])[:, :, :L]
        t = jnp.swapaxes(t, 1, 2)
        return t.reshape((B, S) + t.shape[3:])

    return unsub(num), unsub(jnp.swapaxes(m, 3, 4)), unsub(jnp.swapaxes(den, 3, 4))


def dilated_group_sample(q, k_ext, v_ext, n_hist, window, dilation):
    T, E = q.shape[1], q.shape[-1]
    span = window // dilation
    idx = n_hist + jnp.arange(T)[:, None] - dilation * jnp.arange(span + 1)[None, :]
    valid = idx >= 0
    idx_c = jnp.maximum(idx, 0)
    kg = jnp.take(k_ext, idx_c, axis=1)
    vg = jnp.take(v_ext, idx_c, axis=1)
    s = jnp.einsum('bthe,btjhe->bthj', q, kg, preferred_element_type=F32) * E ** -0.5
    s = jnp.where(valid[None, :, None, :], s, -jnp.inf)
    m = jnp.max(s, axis=-1)
    p = jnp.exp(s - m[..., None])
    den = jnp.sum(p, axis=-1)
    num = jnp.einsum('bthj,btjhe->bthe', p, vg.astype(F32))
    return num, m, den


def combine_groups(parts):
    num = jnp.stack([pt[0] for pt in parts])
    m = jnp.stack([pt[1] for pt in parts])
    den = jnp.stack([pt[2] for pt in parts])
    w = jnp.exp(m - jnp.max(m, axis=0, keepdims=True))
    return jnp.sum(w[..., None] * num, axis=0) / jnp.sum(w * den, axis=0)[..., None]


def odd_mixer_prompt(h, w_qkv, w_out):
    B, S, _ = h.shape
    q, k, v = swa_project(h, jnp.arange(S), w_qkv)
    parts = [dilated_group_prompt(q[:, :, gi], k[:, :, gi], v[:, :, gi], wd, dl)
             for gi, (wd, dl) in enumerate(WIN_CONFIGS)]
    o = combine_groups(parts)
    y = o.reshape(B, S, D_SWA).astype(h.dtype) @ w_out
    rows = [jnp.stack([k[:, S - min(wd, S):, gi], v[:, S - min(wd, S):, gi]], axis=2)
            for gi, (wd, _) in enumerate(WIN_CONFIGS)]
    return y, rows


def odd_mixer_sample(h, bufs, w_qkv, w_out):
    B, T, _ = h.shape
    q, k, v = swa_project(h, PAST_LEN + jnp.arange(T), w_qkv)
    parts, rows = [], []
    for gi, (wd, dl) in enumerate(WIN_CONFIGS):
        buf = bufs[gi]
        k_ext = jnp.concatenate([buf[:, :, 0], k[:, :, gi]], axis=1)
        v_ext = jnp.concatenate([buf[:, :, 1], v[:, :, gi]], axis=1)
        parts.append(dilated_group_sample(q[:, :, gi], k_ext, v_ext, buf.shape[1], wd, dl))
        rows.append(jnp.stack([k[:, :, gi], v[:, :, gi]], axis=2))
    o = combine_groups(parts)
    y = o.reshape(B, T, D_SWA).astype(h.dtype) @ w_out
    return y, rows


def mem_kv(mem, g, w_k, w_v):
    B, N, _ = mem.shape
    mn = rmsnorm(mem, g)
    return ((mn @ w_k).reshape(B, N, MEM_HEADS, MEM_HEAD_DIM),
            (mn @ w_v).reshape(B, N, MEM_HEADS, MEM_HEAD_DIM))


def mem_attend(h, k, v, w_q, w_o):
    B, T, _ = h.shape
    q = (h @ w_q).reshape(B, T, MEM_HEADS, MEM_HEAD_DIM)
    s = jnp.einsum('bthe,bnhe->bhtn', q, k, preferred_element_type=F32) * MEM_HEAD_DIM ** -0.5
    p = jax.nn.softmax(s, axis=-1)
    o = jnp.einsum('bhtn,bnhe->bthe', p, v.astype(F32)).reshape(B, T, D_MODEL).astype(h.dtype)
    return o @ w_o


def conv_ffn(h, hist, w_up, conv_w, w_down):
    u = h @ w_up
    u_ext = jnp.concatenate([hist, u], axis=1)
    c = causal_dwconv(u_ext, conv_w)
    y = (jax.nn.silu(c[..., :D_FF]) * c[..., D_FF:]) @ w_down
    return y, u_ext[:, -(FFN_CONV - 1):]


def setup_inputs(seed: int = 0) -> dict:
    key = jax.random.key(seed)
    ks = iter(jax.random.split(key, 40))

    def nrm(shape, scale=1.0):
        return jax.random.normal(next(ks), shape, F32) * scale

    def gain(shape):
        return 1.0 + 0.02 * jax.random.normal(next(ks), shape, F32)

    d = D_MODEL
    inp = {
        'x_prompt': nrm((BATCH, SEQ, d)),
        'x_sample': nrm((DEC_BATCH, DEC_SEQ, d)),
        'state_pool': nrm((N_EVEN, DEC_BATCH, POOL_HIST, D_POOL)),
        'state_dn_conv': nrm((N_EVEN, DEC_BATCH, DN_CONV - 1, 3 * D_DN)),
        'state_dn': nrm((N_EVEN, DEC_BATCH, DN_HEADS, DN_HEAD_DIM, DN_HEAD_DIM), 0.3),
        'cache_win_w128': nrm((N_ODD, DEC_BATCH, min(WIN_CONFIGS[0][0], PAST_LEN), 2, SWA_HEADS, SWA_HEAD_DIM)),
        'cache_win_w512': nrm((N_ODD, DEC_BATCH, min(WIN_CONFIGS[1][0], PAST_LEN), 2, SWA_HEADS, SWA_HEAD_DIM)),
        'cache_win_w2048': nrm((N_ODD, DEC_BATCH, min(WIN_CONFIGS[2][0], PAST_LEN), 2, SWA_HEADS, SWA_HEAD_DIM)),
        'cache_mem_k': nrm((DEPTH, DEC_BATCH, N_MEM, MEM_HEADS, MEM_HEAD_DIM)),
        'cache_mem_v': nrm((DEPTH, DEC_BATCH, N_MEM, MEM_HEADS, MEM_HEAD_DIM)),
        'state_ffn_conv': nrm((DEPTH, DEC_BATCH, FFN_CONV - 1, 2 * D_FF)),
        'mem_prompt': nrm((BATCH, N_MEM, d)),
        'g_mix': gain((DEPTH, d)),
        'w_in_ab': nrm((N_EVEN, d, D_IN_EVEN), d ** -0.5),
        'w_pool': nrm((N_EVEN, N_POOL_GROUPS, POOL_GROUP_DIM, POOL_GROUP_DIM), POOL_GROUP_DIM ** -0.5),
        'pool_scale': gain((N_EVEN, D_POOL)),
        'dn_conv_w': nrm((N_EVEN, DN_CONV, 3 * D_DN), DN_CONV ** -0.5),
        'dn_a_log': jnp.log(jax.random.uniform(next(ks), (N_EVEN, DN_HEADS), F32, 1.0, 16.0)),
        'dn_dt_bias': jnp.log(jnp.expm1(jax.random.uniform(next(ks), (N_EVEN, DN_HEADS), F32, 1e-3, 1e-1))),
        'dn_norm_w': gain((N_EVEN, DN_HEAD_DIM)),
        'w_out_ab': nrm((N_EVEN, D_MIX_EVEN, d), D_MIX_EVEN ** -0.5),
        'w_qkv_c': nrm((N_ODD, d, N_WIN * 3 * D_SWA), d ** -0.5),
        'w_out_c': nrm((N_ODD, D_SWA, d), D_SWA ** -0.5),
        'g_mem_q': gain((DEPTH, d)),
        'g_mem_kv': gain((DEPTH, d)),
        'w_mem_q': nrm((DEPTH, d, d), d ** -0.5),
        'w_mem_k': nrm((DEPTH, d, d), d ** -0.5),
        'w_mem_v': nrm((DEPTH, d, d), d ** -0.5),
        'w_mem_o': nrm((DEPTH, d, d), d ** -0.5),
        'g_ffn': gain((DEPTH, d)),
        'w_up': nrm((DEPTH, d, 2 * D_FF), d ** -0.5),
        'ffn_conv_w': nrm((DEPTH, FFN_CONV, 2 * D_FF), FFN_CONV ** -0.5),
        'w_down': nrm((DEPTH, D_FF, d), D_FF ** -0.5),
        'g_final': gain((d,)),
    }
    return inp


def reference(x_prompt, x_sample, state_pool, state_dn_conv, state_dn, cache_win_w128, cache_win_w512,
              cache_win_w2048, cache_mem_k, cache_mem_v, state_ffn_conv, mem_prompt, g_mix, w_in_ab, w_pool,
              pool_scale, dn_conv_w, dn_a_log, dn_dt_bias, dn_norm_w, w_out_ab, w_qkv_c, w_out_c, g_mem_q,
              g_mem_kv, w_mem_q, w_mem_k, w_mem_v, w_mem_o, g_ffn, w_up, ffn_conv_w, w_down, g_final):
    xp, xs = x_prompt, x_sample
    Bp = xp.shape[0]
    dt = xp.dtype
    win_caches = (cache_win_w128, cache_win_w512, cache_win_w2048)
    pool_p, pool_s, dconv_p, dconv_s, dn_p, dn_s = [], [], [], [], [], []
    win_p = [[] for _ in WIN_CONFIGS]
    win_s = [[] for _ in WIN_CONFIGS]
    memk_p, memv_p, fconv_p, fconv_s = [], [], [], []
    for layer in range(DEPTH):
        hp = rmsnorm(xp, g_mix[layer])
        hs = rmsnorm(xs, g_mix[layer])
        if layer % 2 == 0:
            e = layer // 2
            wts = (w_in_ab[e], w_pool[e], pool_scale[e], dn_conv_w[e], dn_a_log[e], dn_dt_bias[e],
                   dn_norm_w[e], w_out_ab[e])
            yp, a_p, c_p, s_p = even_mixer(
                hp, jnp.zeros((Bp, 0, D_POOL), dt), jnp.zeros((Bp, DN_CONV - 1, 3 * D_DN), dt),
                jnp.zeros((Bp, DN_HEADS, DN_HEAD_DIM, DN_HEAD_DIM), F32), 0, *wts)
            ys, a_s, c_s, s_s = even_mixer(hs, state_pool[e], state_dn_conv[e], state_dn[e], PAST_LEN, *wts)
            pool_p.append(a_p); pool_s.append(a_s)
            dconv_p.append(c_p); dconv_s.append(c_s)
            dn_p.append(s_p); dn_s.append(s_s)
        else:
            o = layer // 2
            yp, rows_p = odd_mixer_prompt(hp, w_qkv_c[o], w_out_c[o])
            ys, rows_s = odd_mixer_sample(hs, [c[o] for c in win_caches], w_qkv_c[o], w_out_c[o])
            for gi in range(N_WIN):
                win_p[gi].append(rows_p[gi]); win_s[gi].append(rows_s[gi])
        xp = xp + yp
        xs = xs + ys
        mk, mv = mem_kv(mem_prompt, g_mem_kv[layer], w_mem_k[layer], w_mem_v[layer])
        memk_p.append(mk); memv_p.append(mv)
        xp = xp + mem_attend(rmsnorm(xp, g_mem_q[layer]), mk, mv, w_mem_q[layer], w_mem_o[layer])
        xs = xs + mem_attend(rmsnorm(xs, g_mem_q[layer]), cache_mem_k[layer], cache_mem_v[layer],
                             w_mem_q[layer], w_mem_o[layer])
        yp, f_p = conv_ffn(rmsnorm(xp, g_ffn[layer]), jnp.zeros((Bp, FFN_CONV - 1, 2 * D_FF), dt),
                           w_up[layer], ffn_conv_w[layer], w_down[layer])
        ys, f_s = conv_ffn(rmsnorm(xs, g_ffn[layer]), state_ffn_conv[layer], w_up[layer], ffn_conv_w[layer],
                           w_down[layer])
        fconv_p.append(f_p); fconv_s.append(f_s)
        xp = xp + yp
        xs = xs + ys
    y_prompt = rmsnorm(xp, g_final)
    y_sample = rmsnorm(xs, g_final)
    return (y_prompt, y_sample,
            jnp.stack(pool_p), jnp.stack(pool_s),
            jnp.stack(dconv_p), jnp.stack(dconv_s),
            jnp.stack(dn_p), jnp.stack(dn_s),
            jnp.stack(win_p[0]), jnp.stack(win_s[0]),
            jnp.stack(win_p[1]), jnp.stack(win_s[1]),
            jnp.stack(win_p[2]), jnp.stack(win_s[2]),
            jnp.stack(memk_p), jnp.stack(memv_p),
            jnp.stack(fconv_p), jnp.stack(fconv_s))
```

```python
import functools

import jax
import jax.numpy as jnp
from jax import lax
from jax.experimental import pallas as pl
from jax.experimental.pallas import tpu as pltpu

F32 = jnp.float32
BF16 = jnp.bfloat16
EPS = 1e-6

PAST_LEN = 8192
POOL_WINDOWS = (2, 4, 8, 16)
POOL_HALO = 16
DN_HEADS = 4
DN_DIM = 128
DN_CONV = 4
DN_CHUNK = 128
DN_HALO = 8
WIN_CONFIGS = ((128, 1), (512, 4), (2048, 16))
SWA_HEADS = 8
SWA_DIM = 64
D_SWA = SWA_HEADS * SWA_DIM
SWA_BLOCK = 128
SWA_PACK = D_SWA + 256
ROT_HALF = 8
ROPE_THETA = 500000.0
MEM_HEADS = 4
FFN_CHUNK = 256
LANES = 128
VMEM_LIMIT_BYTES = 56 * 1024 * 1024


def _cparams(*sem):
    return pltpu.CompilerParams(dimension_semantics=sem, vmem_limit_bytes=VMEM_LIMIT_BYTES)


def _rms(x, g):
    return x * lax.rsqrt(jnp.mean(x * x, axis=-1, keepdims=True) + EPS) * g


def _sigmoid(x):
    return 1.0 / (1.0 + jnp.exp(-x))


def _dot(a, b):
    return jnp.dot(a, b, preferred_element_type=F32)


def _dot_nt(a, b):
    return lax.dot_general(a, b, (((1,), (1,)), ((), ())), preferred_element_type=F32)


def _dot_tn(a, b):
    return lax.dot_general(a, b, (((0,), (0,)), ((), ())), preferred_element_type=F32)


def _norm_proj_kernel(x_ref, g_ref, *refs, n_w, tn):
    w_refs, o_refs = refs[:n_w], refs[n_w:]
    h = _rms(x_ref[...], g_ref[...]).astype(BF16)
    for w_ref, o_ref in zip(w_refs, o_refs):
        n = w_ref.shape[1]
        for j in range(0, n, tn):
            jw = min(tn, n - j)
            o_ref[:, j:j + jw] = _dot(h, w_ref[:, j:j + jw]).astype(o_ref.dtype)


def _norm_proj(x, g, ws, out_dtypes, tm):
    m, d = x.shape
    tm = min(tm, m)
    in_specs = [pl.BlockSpec((tm, d), lambda i: (i, 0)), pl.BlockSpec((1, d), lambda i: (0, 0))]
    in_specs += [pl.BlockSpec(w.shape, lambda i: (0, 0)) for w in ws]
    return pl.pallas_call(
        functools.partial(_norm_proj_kernel, n_w=len(ws), tn=512),
        grid=(m // tm,),
        in_specs=in_specs,
        out_specs=[pl.BlockSpec((tm, w.shape[1]), lambda i: (i, 0)) for w in ws],
        out_shape=[jax.ShapeDtypeStruct((m, w.shape[1]), dt) for w, dt in zip(ws, out_dtypes)],
        compiler_params=_cparams("parallel"),
        name="norm_proj",
    )(x, g.reshape(1, d), *ws)


def _qkv_rope_kernel(x_ref, g_ref, w_ref, cos_ref, sa_ref, sb_ref, o_ref):
    h = _rms(x_ref[...], g_ref[...]).astype(BF16)
    cos, sa, sb = cos_ref[...], sa_ref[...], sb_ref[...]
    for piece in range(w_ref.shape[1] // D_SWA):
        c0 = piece * D_SWA
        y = _dot(h, w_ref[:, c0:c0 + D_SWA])
        if piece % 3 == 2:
            o_ref[:, c0:c0 + D_SWA] = y
            continue
        for a in range(D_SWA // LANES):
            ya = y[:, a * LANES:(a + 1) * LANES]
            rot = (ya * cos + pltpu.roll(ya, LANES - ROT_HALF, axis=1) * sa
                   + pltpu.roll(ya, ROT_HALF, axis=1) * sb)
            o_ref[:, c0 + a * LANES:c0 + (a + 1) * LANES] = rot


def _rope_tables(pos):
    t = pos.shape[0]
    inv_freq = ROPE_THETA ** (-jnp.arange(ROT_HALF, dtype=F32) / ROT_HALF)
    ang = pos.astype(F32)[:, None] * inv_freq[None, :]
    cos, sin = jnp.cos(ang), jnp.sin(ang)
    rest = SWA_DIM - 2 * ROT_HALF
    c64 = jnp.concatenate([cos, cos, jnp.ones((t, rest), F32)], axis=1)
    a64 = jnp.concatenate([-sin, jnp.zeros((t, SWA_DIM - ROT_HALF), F32)], axis=1)
    b64 = jnp.concatenate([jnp.zeros((t, ROT_HALF), F32), sin, jnp.zeros((t, rest), F32)], axis=1)
    return tuple(jnp.concatenate([u, u], axis=1) for u in (c64, a64, b64))


def _qkv_rope(x, g, w, tables, tm):
    m, d = x.shape
    n = w.shape[1]
    tm = min(tm, m)
    nper = tables[0].shape[0] // tm
    tab_spec = pl.BlockSpec((tm, LANES), lambda i: (i % nper, 0))
    return pl.pallas_call(
        _qkv_rope_kernel,
        grid=(m // tm,),
        in_specs=[pl.BlockSpec((tm, d), lambda i: (i, 0)), pl.BlockSpec((1, d), lambda i: (0, 0)),
                  pl.BlockSpec((d, n), lambda i: (0, 0)), tab_spec, tab_spec, tab_spec],
        out_specs=pl.BlockSpec((tm, n), lambda i: (i, 0)),
        out_shape=jax.ShapeDtypeStruct((m, n), F32),
        compiler_params=_cparams("parallel"),
        name="qkv_rope",
    )(x, g.reshape(1, d), w, *tables)


def _proj_res_kernel(*refs, n_in):
    a_refs, w_refs = refs[:n_in], refs[n_in:2 * n_in]
    x_ref, o_ref = refs[2 * n_in], refs[2 * n_in + 1]
    acc = x_ref[...]
    for a_ref, w_ref in zip(a_refs, w_refs):
        acc = acc + _dot(a_ref[...], w_ref[...])
    o_ref[...] = acc


def _proj_res(acts, ws, x, tm):
    m, d = x.shape
    tm = min(tm, m)
    in_specs = [pl.BlockSpec((tm, a.shape[1]), lambda i: (i, 0)) for a in acts]
    in_specs += [pl.BlockSpec(w.shape, lambda i: (0, 0)) for w in ws]
    in_specs += [pl.BlockSpec((tm, d), lambda i: (i, 0))]
    return pl.pallas_call(
        functools.partial(_proj_res_kernel, n_in=len(acts)),
        grid=(m // tm,),
        in_specs=in_specs,
        out_specs=pl.BlockSpec((tm, d), lambda i: (i, 0)),
        out_shape=jax.ShapeDtypeStruct((m, d), F32),
        compiler_params=_cparams("parallel"),
        name="proj_res",
    )(*acts, *ws, x)


def _pool_kernel(a_ref, hist_ref, w_ref, sc_ref, o_ref, buf, *, tm, p0):
    i = pl.program_id(1)

    @pl.when(i == 0)
    def _():
        buf[0:POOL_HALO, :] = hist_ref[0]

    @pl.when(i > 0)
    def _():
        buf[0:POOL_HALO, :] = buf[tm:tm + POOL_HALO, :]

    buf[POOL_HALO:POOL_HALO + tm, :] = a_ref[0]
    pos = p0 + i * tm + lax.broadcasted_iota(jnp.int32, (tm, 1), 0)
    for gi, win in enumerate(POOL_WINDOWS):
        cs = slice(gi * LANES, (gi + 1) * LANES)
        cur = buf[POOL_HALO:POOL_HALO + tm, cs]
        tot = cur
        for j in range(1, win):
            tot = tot + buf[POOL_HALO - j:POOL_HALO - j + tm, cs]
        cnt = jnp.minimum(pos + 1, win).astype(F32)
        z = tot / cnt - cur
        y = _dot(z.astype(BF16), w_ref[gi]) * sc_ref[:, cs]
        o_ref[0, :, cs] = y.astype(o_ref.dtype)


def _pool(proj, hist, w, scale, tm, p0):
    b, t, _ = proj.shape
    dp = len(POOL_WINDOWS) * LANES
    tm = min(tm, t)
    return pl.pallas_call(
        functools.partial(_pool_kernel, tm=tm, p0=p0),
        grid=(b, t // tm),
        in_specs=[pl.BlockSpec((1, tm, dp), lambda bi, i: (bi, i, 0)),
                  pl.BlockSpec((1, POOL_HALO, dp), lambda bi, i: (bi, 0, 0)),
                  pl.BlockSpec(w.shape, lambda bi, i: (0, 0, 0)),
                  pl.BlockSpec((1, dp), lambda bi, i: (0, 0))],
        out_specs=pl.BlockSpec((1, tm, dp), lambda bi, i: (bi, i, 0)),
        out_shape=jax.ShapeDtypeStruct((b, t, dp), BF16),
        scratch_shapes=[pltpu.VMEM((POOL_HALO + tm, dp), F32)],
        compiler_params=_cparams("parallel", "arbitrary"),
        name="pool_mix",
    )(proj, hist, w, scale.reshape(1, dp))


def _cumsum_rows(x):
    n = x.shape[0]
    row = lax.broadcasted_iota(jnp.int32, x.shape, 0)
    sh = 1
    while sh < n:
        x = x + jnp.where(row >= sh, pltpu.roll(x, sh, axis=0), 0.0)
        sh *= 2
    return x


def _delta_kernel(q_ref, k_ref, v_ref, gate_ref, ba_ref, hist_ref, s0_ref, cw_ref, alog_ref, dtb_ref,
                  nw_ref, o_ref, s_ref, ext, *, rows):
    c = pl.program_id(1)
    C = DN_CHUNK
    dd = DN_HEADS * DN_DIM

    @pl.when(c == 0)
    def _():
        ext[0:DN_HALO, :] = hist_ref[0]
        s_ref[0] = s0_ref[0]

    ext[DN_HALO:DN_HALO + rows, 0:dd] = q_ref[0]
    ext[DN_HALO:DN_HALO + rows, dd:2 * dd] = k_ref[0]
    ext[DN_HALO:DN_HALO + rows, 2 * dd:3 * dd] = v_ref[0]
    if rows < C:
        ext[DN_HALO + rows:DN_HALO + C, :] = jnp.zeros((C - rows, 3 * dd), F32)
    conv = cw_ref[DN_CONV - 1:DN_CONV, :] * ext[DN_HALO:DN_HALO + C, :]
    for kk in range(DN_CONV - 1):
        off = DN_HALO - (DN_CONV - 1) + kk
        conv = conv + cw_ref[kk:kk + 1, :] * ext[off:off + C, :]
    act = conv * _sigmoid(conv)
    ext[0:DN_HALO, :] = ext[rows:rows + DN_HALO, :]

    ba = ba_ref[0]
    gate = gate_ref[0]
    if rows < C:
        ba = jnp.concatenate([ba, jnp.zeros((C - rows, LANES), F32)], axis=0)
    valid = lax.broadcasted_iota(jnp.int32, (C, LANES), 0) < rows
    beta_t = jnp.where(valid, _sigmoid(ba), 0.0)
    xg = ba + dtb_ref[...]
    softplus = jnp.maximum(xg, 0.0) + jnp.log(1.0 + jnp.exp(-jnp.abs(xg)))
    g_t = jnp.where(valid, -jnp.exp(alog_ref[...]) * softplus, 0.0)
    gcum = _cumsum_rows(g_t)
    gcum_t = gcum.T
    e_g = jnp.exp(gcum)
    g_last = gcum[C - 1:C, :]
    e_rev = jnp.exp(g_last - gcum)
    e_last = jnp.exp(g_last)

    ri = lax.broadcasted_iota(jnp.int32, (C, C), 0)
    ci = lax.broadcasted_iota(jnp.int32, (C, C), 1)
    incl = ri >= ci
    strict = ri > ci
    eye = (ri == ci).astype(F32)

    for h in range(DN_HEADS):
        hs = slice(h * DN_DIM, (h + 1) * DN_DIM)
        qh = act[:, hs]
        kh = act[:, dd + h * DN_DIM:dd + (h + 1) * DN_DIM]
        vh = act[:, 2 * dd + h * DN_DIM:2 * dd + (h + 1) * DN_DIM]
        qn = qh * lax.rsqrt(jnp.sum(qh * qh, axis=-1, keepdims=True) + EPS) * (DN_DIM ** -0.5)
        kn = kh * lax.rsqrt(jnp.sum(kh * kh, axis=-1, keepdims=True) + EPS)
        beta = beta_t[:, h:h + 1]
        gcol = gcum[:, DN_HEADS + h:DN_HEADS + h + 1]
        grow = gcum_t[DN_HEADS + h:DN_HEADS + h + 1, :]
        decay = jnp.where(incl, jnp.exp(gcol - grow), 0.0)
        eg = e_g[:, DN_HEADS + h:DN_HEADS + h + 1]
        er = e_rev[:, DN_HEADS + h:DN_HEADS + h + 1]
        el = e_last[:, DN_HEADS + h:DN_HEADS + h + 1]

        kb = kn.astype(BF16)
        a = jnp.where(strict, _dot_nt(kb, kb) * decay * beta, 0.0)
        a_blk = jnp.where((ri >> 4) == (ci >> 4), a, 0.0)
        p = eye - a_blk
        xp = a_blk
        for _ in range(3):
            xb = xp.astype(BF16)
            xp = _dot(xb, xb)
            p = p + _dot(p.astype(BF16), xp.astype(BF16))
        sh = 4
        while (1 << sh) < C:
            off = jnp.where(((ri >> (sh + 1)) == (ci >> (sh + 1))) & ((ri >> sh) != (ci >> sh)), a, 0.0)
            pb = p.astype(BF16)
            p = p - _dot(_dot(pb, off.astype(BF16)).astype(BF16), pb)
            sh += 1
        pb = p.astype(BF16)
        u = _dot(pb, (vh * beta).astype(BF16))
        w = _dot(pb, (kn * (beta * eg)).astype(BF16))
        qk = _dot_nt(qn.astype(BF16), kb) * decay
        qg = (qn * eg).astype(BF16)
        kg = (kn * er).astype(BF16)

        s = s_ref[0, h]
        sb = s.astype(BF16)
        v_new = u - _dot(w.astype(BF16), sb)
        vb = v_new.astype(BF16)
        o = _dot(qg, sb) + _dot(qk.astype(BF16), vb)
        s_ref[0, h] = s * el + _dot_tn(kg, vb)

        o = o * lax.rsqrt(jnp.mean(o * o, axis=-1, keepdims=True) + EPS) * nw_ref[...]
        o = o[0:rows] * (gate[:, hs] * _sigmoid(gate[:, hs]))
        o_ref[0, :, hs] = o.astype(o_ref.dtype)


def _delta(proj, hist, s0, conv_w, alog_row, dtb_row, norm_w):
    b, t, _ = proj.shape
    dd = DN_HEADS * DN_DIM
    rows = min(DN_CHUNK, t)
    col = lambda j: pl.BlockSpec((1, rows, dd), lambda bi, c: (bi, c, j))
    full2 = lambda a: pl.BlockSpec(a.shape, lambda bi, c: (0, 0))
    return pl.pallas_call(
        functools.partial(_delta_kernel, rows=rows),
        grid=(b, t // rows),
        in_specs=[col(1), col(2), col(3), col(4),
                  pl.BlockSpec((1, rows, LANES), lambda bi, c: (bi, c, 5 * dd // LANES)),
                  pl.BlockSpec((1, DN_HALO, 3 * dd), lambda bi, c: (bi, 0, 0)),
                  pl.BlockSpec((1, DN_HEADS, DN_DIM, DN_DIM), lambda bi, c: (bi, 0, 0, 0)),
                  full2(conv_w), full2(alog_row), full2(dtb_row), full2(norm_w)],
        out_specs=[pl.BlockSpec((1, rows, dd), lambda bi, c: (bi, c, 0)),
                   pl.BlockSpec((1, DN_HEADS, DN_DIM, DN_DIM), lambda bi, c: (bi, 0, 0, 0))],
        out_shape=[jax.ShapeDtypeStruct((b, t, dd), BF16),
                   jax.ShapeDtypeStruct((b, DN_HEADS, DN_DIM, DN_DIM), F32)],
        scratch_shapes=[pltpu.VMEM((DN_HALO + DN_CHUNK, 3 * dd), F32)],
        compiler_params=_cparams("parallel", "arbitrary"),
        name="delta_rule",
    )(proj, proj, proj, proj, proj, hist, s0, conv_w, alog_row, dtb_row, norm_w)


def _swa_group_kernel(q_ref, k_ref, v_ref, o_ref, kprev, vprev):
    n = pl.program_id(2)

    @pl.when(n == 0)
    def _():
        kprev[...] = jnp.zeros_like(kprev)
        vprev[...] = jnp.zeros_like(vprev)

    q = q_ref[0].astype(BF16)
    k = k_ref[0].astype(BF16)
    v = v_ref[0].astype(BF16)
    blk = SWA_BLOCK
    qi = lax.broadcasted_iota(jnp.int32, (blk, blk), 0)
    kj = lax.broadcasted_iota(jnp.int32, (blk, blk), 1)
    valid_prev = (kj >= qi) & (n > 0)
    valid_cur = kj <= qi
    lane = lax.broadcasted_iota(jnp.int32, (1, LANES), 1)
    lane16 = lax.broadcasted_iota(jnp.int32, (blk, LANES), 1) >> 4
    m_tile = jnp.zeros((blk, LANES), F32)
    d_tile = jnp.zeros((blk, LANES), F32)
    scale = SWA_DIM ** -0.5
    zero = jnp.zeros((), BF16)
    for pair in range(D_SWA // LANES):
        sl = slice(pair * LANES, (pair + 1) * LANES)
        qp, kc, vc = q[:, sl], k[:, sl], v[:, sl]
        kp, vp = kprev[:, sl], vprev[:, sl]
        acc = jnp.zeros((blk, LANES), F32)
        for half in range(2):
            hm = (lane < SWA_DIM) if half == 0 else (lane >= SWA_DIM)
            qh = jnp.where(hm, qp, zero)
            s_p = jnp.where(valid_prev, _dot_nt(qh, kp) * scale, -jnp.inf)
            s_c = jnp.where(valid_cur, _dot_nt(qh, kc) * scale, -jnp.inf)
            m = jnp.maximum(jnp.max(s_p, axis=-1, keepdims=True), jnp.max(s_c, axis=-1, keepdims=True))
            e_p = jnp.exp(s_p - m)
            e_c = jnp.exp(s_c - m)
            den = jnp.sum(e_p, axis=-1, keepdims=True) + jnp.sum(e_c, axis=-1, keepdims=True)
            acc = acc + _dot(e_p.astype(BF16), jnp.where(hm, vp, zero))
            acc = acc + _dot(e_c.astype(BF16), jnp.where(hm, vc, zero))
            head = 2 * pair + half
            m_tile = jnp.where(lane16 == head, m, m_tile)
            d_tile = jnp.where(lane16 == head, den, d_tile)
        o_ref[0, :, sl] = acc
    o_ref[0, :, D_SWA:D_SWA + LANES] = m_tile
    o_ref[0, :, D_SWA + LANES:D_SWA + 2 * LANES] = d_tile
    kprev[...] = k
    vprev[...] = v


def _swa_group(qkv, gi, dil):
    b, s, c = qkv.shape
    npiece = c // D_SWA
    sub_len = s // dil
    nb = sub_len // SWA_BLOCK
    view = qkv.reshape(b, sub_len, dil * c)
    piece = lambda j: pl.BlockSpec((1, SWA_BLOCK, D_SWA), lambda bi, r, n: (bi, n, r * npiece + 3 * gi + j))
    out = pl.pallas_call(
        _swa_group_kernel,
        grid=(b, dil, nb),
        in_specs=[piece(0), piece(1), piece(2)],
        out_specs=pl.BlockSpec((1, SWA_BLOCK, SWA_PACK), lambda bi, r, n: (bi, n, r)),
        out_shape=jax.ShapeDtypeStruct((b, sub_len, dil * SWA_PACK), F32),
        scratch_shapes=[pltpu.VMEM((SWA_BLOCK, D_SWA), BF16), pltpu.VMEM((SWA_BLOCK, D_SWA), BF16)],
        compiler_params=_cparams("parallel", "parallel", "arbitrary"),
        name="swa_group",
    )(view, view, view)
    return out.reshape(b * s, SWA_PACK)


def _swa_merge_kernel(p0_ref, p1_ref, p2_ref, ex_ref, w_ref, x_ref, o_ref):
    parts = (p0_ref, p1_ref, p2_ref)
    ms = [p[:, D_SWA:D_SWA + LANES] for p in parts]
    dens = [p[:, D_SWA + LANES:D_SWA + 2 * LANES] for p in parts]
    mx = jnp.maximum(jnp.maximum(ms[0], ms[1]), ms[2])
    ws = [jnp.exp(m - mx) for m in ms]
    tot = ws[0] * dens[0] + ws[1] * dens[1] + ws[2] * dens[2]
    o = jnp.zeros((x_ref.shape[0], D_SWA), F32)
    for p, wg in zip(parts, ws):
        coef = jnp.dot(wg / tot, ex_ref[...], preferred_element_type=F32, precision=lax.Precision.HIGHEST)
        o = o + coef * p[:, 0:D_SWA]
    o_ref[...] = x_ref[...] + _dot(o.astype(BF16), w_ref[...])


def _swa_merge(packs, w, x, tm):
    m, d = x.shape
    lane = jnp.arange(LANES)[:, None]
    colh = jnp.arange(D_SWA)[None, :] // SWA_DIM
    expand = (lane == colh * (LANES // SWA_HEADS)).astype(F32)
    pspec = pl.BlockSpec((tm, SWA_PACK), lambda i: (i, 0))
    return pl.pallas_call(
        _swa_merge_kernel,
        grid=(m // tm,),
        in_specs=[pspec, pspec, pspec, pl.BlockSpec(expand.shape, lambda i: (0, 0)),
                  pl.BlockSpec(w.shape, lambda i: (0, 0)), pl.BlockSpec((tm, d), lambda i: (i, 0))],
        out_specs=pl.BlockSpec((tm, d), lambda i: (i, 0)),
        out_shape=jax.ShapeDtypeStruct((m, d), F32),
        compiler_params=_cparams("parallel"),
        name="swa_merge",
    )(*packs, expand, w, x)


def _swa_sample_kernel(qkv_ref, c0_ref, c1_ref, c2_ref, o_ref, *scr, t_new):
    caches = (c0_ref, c1_ref, c2_ref)
    rows = SWA_HEADS * t_new
    rh = jnp.concatenate([jnp.full((t_new, D_SWA), h, jnp.int32) for h in range(SWA_HEADS)], axis=0)
    chd = lax.broadcasted_iota(jnp.int32, (rows, D_SWA), 1) >> 6
    head_mask = rh == chd
    scores, vals = [], []
    for gi, (win, dil) in enumerate(WIN_CONFIGS):
        kall, vall = scr[2 * gi], scr[2 * gi + 1]
        n_hist = caches[gi].shape[1]
        base = gi * 3 * D_SWA
        q = qkv_ref[0, :, base:base + D_SWA]
        kall[0:n_hist, :] = caches[gi][0, :, 0:D_SWA].astype(BF16)
        vall[0:n_hist, :] = caches[gi][0, :, D_SWA:2 * D_SWA].astype(BF16)
        pad = jnp.zeros((LANES - t_new, D_SWA), BF16)
        kall[n_hist:n_hist + t_new, :] = qkv_ref[0, :, base + D_SWA:base + 2 * D_SWA].astype(BF16)
        vall[n_hist:n_hist + t_new, :] = qkv_ref[0, :, base + 2 * D_SWA:base + 3 * D_SWA].astype(BF16)
        kall[n_hist + t_new:n_hist + LANES, :] = pad
        vall[n_hist + t_new:n_hist + LANES, :] = pad
        qrows = jnp.where(head_mask, jnp.concatenate([q] * SWA_HEADS, axis=0), 0.0).astype(BF16)
        s = _dot_nt(qrows, kall[...]) * (SWA_DIM ** -0.5)
        nk = n_hist + LANES
        tok = jnp.concatenate([lax.broadcasted_iota(jnp.int32, (t_new, nk), 0)] * SWA_HEADS, axis=0)
        idx = lax.broadcasted_iota(jnp.int32, (rows, nk), 1)
        dist = n_hist + tok - idx
        valid = (dist >= 0) & (dist <= win) & ((dist & (dil - 1)) == 0)
        scores.append(jnp.where(valid, s, -jnp.inf))
        vals.append(vall)
    m = scores[0].max(axis=-1, keepdims=True)
    for s in scores[1:]:
        m = jnp.maximum(m, s.max(axis=-1, keepdims=True))
    den = jnp.zeros((rows, 1), F32)
    num = jnp.zeros((rows, D_SWA), F32)
    for s, vall in zip(scores, vals):
        e = jnp.exp(s - m)
        den = den + jnp.sum(e, axis=-1, keepdims=True)
        num = num + _dot(e.astype(BF16), vall[...])
    o = jnp.where(head_mask, num / den, 0.0)
    out = o[0:t_new]
    for h in range(1, SWA_HEADS):
        out = out + o[h * t_new:(h + 1) * t_new]
    o_ref[0] = out.astype(o_ref.dtype)


def _swa_sample(qkv, caches):
    b, t, c = qkv.shape
    scratch = []
    for ch in caches:
        scratch += [pltpu.VMEM((ch.shape[1] + LANES, D_SWA), BF16)] * 2
    return pl.pallas_call(
        functools.partial(_swa_sample_kernel, t_new=t),
        grid=(b,),
        in_specs=[pl.BlockSpec((1, t, c), lambda bi: (bi, 0, 0))]
        + [pl.BlockSpec((1,) + ch.shape[1:], lambda bi: (bi, 0, 0)) for ch in caches],
        out_specs=pl.BlockSpec((1, t, D_SWA), lambda bi: (bi, 0, 0)),
        out_shape=jax.ShapeDtypeStruct((b, t, D_SWA), BF16),
        scratch_shapes=scratch,
        compiler_params=_cparams("arbitrary"),
        name="swa_sample",
    )(qkv, *caches)


def _mem_attn_kernel(q_ref, k_ref, v_ref, o_ref):
    d = q_ref.shape[2]
    hd = d // MEM_HEADS
    for h in range(MEM_HEADS):
        sl = slice(h * hd, (h + 1) * hd)
        s = _dot_nt(q_ref[0, :, sl], k_ref[0, :, sl].astype(BF16)) * (hd ** -0.5)
        e = jnp.exp(s - jnp.max(s, axis=-1, keepdims=True))
        o = _dot(e.astype(BF16), v_ref[0, :, sl].astype(BF16)) / jnp.sum(e, axis=-1, keepdims=True)
        o_ref[0, :, sl] = o.astype(o_ref.dtype)


def _mem_attn(q, k, v, tm):
    b, t, d = q.shape
    n = k.shape[1]
    tm = min(tm, t)
    return pl.pallas_call(
        _mem_attn_kernel,
        grid=(b, t // tm),
        in_specs=[pl.BlockSpec((1, tm, d), lambda bi, i: (bi, i, 0)),
                  pl.BlockSpec((1, n, d), lambda bi, i: (bi, 0, 0)),
                  pl.BlockSpec((1, n, d), lambda bi, i: (bi, 0, 0))],
        out_specs=pl.BlockSpec((1, tm, d), lambda bi, i: (bi, i, 0)),
        out_shape=jax.ShapeDtypeStruct((b, t, d), BF16),
        compiler_params=_cparams("parallel", "parallel"),
        name="mem_attn",
    )(q, k, v)


def _ffn_kernel(x_ref, g_ref, wup_ref, cw_ref, wdn_ref, hist_ref, *rest, tm, halo_rows, step, final):
    if final:
        gf_ref, o_ref, st_ref, u_scr, halo, acc = rest
    else:
        o_ref, st_ref, u_scr, halo, acc = rest
    i = pl.program_id(1)
    dff = wdn_ref.shape[0]
    fc = FFN_CHUNK

    @pl.when(i == 0)
    def _():
        halo[...] = hist_ref[0]

    x = x_ref[0]
    h = _rms(x, g_ref[...]).astype(BF16)
    acc[...] = x
    for c in range(dff // fc):
        conv = []
        for half in range(2):
            cols = slice(half * dff + c * fc, half * dff + (c + 1) * fc)
            u = _dot(h, wup_ref[:, cols])
            u_scr[half, 0:halo_rows, :] = halo[:, cols]
            u_scr[half, halo_rows:halo_rows + tm, :] = u
            o1 = halo_rows - step
            o2 = halo_rows - 2 * step
            conv.append(cw_ref[2:3, cols] * u + cw_ref[1:2, cols] * u_scr[half, o1:o1 + tm, :]
                        + cw_ref[0:1, cols] * u_scr[half, o2:o2 + tm, :])
            halo[:, cols] = u_scr[half, tm:tm + halo_rows, :]
        y = (conv[0] * _sigmoid(conv[0]) * conv[1]).astype(BF16)
        acc[...] += _dot(y, wdn_ref[c * fc:(c + 1) * fc, :])
    out = acc[...]
    if final:
        out = _rms(out, gf_ref[...])
    o_ref[0] = out
    st_ref[0] = halo[...]


def _ffn(x, g, w_up, conv_w, w_down, hist, tm, step, g_final=None):
    b, t, d = x.shape
    dff = w_down.shape[0]
    halo_rows = hist.shape[1]
    tm = min(tm, t)
    const = lambda a: pl.BlockSpec(a.shape, lambda bi, i: (0,) * a.ndim, pipeline_mode=pl.Buffered(1))
    g2 = g.reshape(1, d)
    in_specs = [pl.BlockSpec((1, tm, d), lambda bi, i: (bi, i, 0)), const(g2), const(w_up), const(conv_w),
                const(w_down), pl.BlockSpec((1, halo_rows, 2 * dff), lambda bi, i: (bi, 0, 0))]
    args = [x, g2, w_up, conv_w, w_down, hist]
    if g_final is not None:
        gf = g_final.reshape(1, d)
        in_specs.append(const(gf))
        args.append(gf)
    return pl.pallas_call(
        functools.partial(_ffn_kernel, tm=tm, halo_rows=halo_rows, step=step, final=g_final is not None),
        grid=(b, t // tm),
        in_specs=in_specs,
        out_specs=[pl.BlockSpec((1, tm, d), lambda bi, i: (bi, i, 0)),
                   pl.BlockSpec((1, halo_rows, 2 * dff), lambda bi, i: (bi, 0, 0))],
        out_shape=[jax.ShapeDtypeStruct((b, t, d), F32), jax.ShapeDtypeStruct((b, halo_rows, 2 * dff), F32)],
        scratch_shapes=[pltpu.VMEM((2, halo_rows + tm, FFN_CHUNK), F32), pltpu.VMEM((halo_rows, 2 * dff), F32),
                        pltpu.VMEM((tm, d), F32)],
        compiler_params=_cparams("parallel", "arbitrary"),
        name="conv_ffn",
    )(*args)


def _pad_rows_top(a, rows):
    return jnp.pad(a, ((0, 0), (rows - a.shape[1], 0), (0, 0)))


def kernel(x_prompt, x_sample, state_pool, state_dn_conv, state_dn, cache_win_w128, cache_win_w512, cache_win_w2048, cache_mem_k, cache_mem_v, state_ffn_conv, mem_prompt, g_mix, w_in_ab, w_pool, pool_scale, dn_conv_w, dn_a_log, dn_dt_bias, dn_norm_w, w_out_ab, w_qkv_c, w_out_c, g_mem_q, g_mem_kv, w_mem_q, w_mem_k, w_mem_v, w_mem_o, g_ffn, w_up, ffn_conv_w, w_down, g_final):
    bp, s, d = x_prompt.shape
    bs, ts, _ = x_sample.shape
    depth = g_mix.shape[0]
    n_mem = mem_prompt.shape[1]
    dff = w_down.shape[1]
    d_pool = len(POOL_WINDOWS) * LANES
    dd = DN_HEADS * DN_DIM
    win_caches = (cache_win_w128, cache_win_w512, cache_win_w2048)

    xp = x_prompt.reshape(bp * s, d)
    xs = x_sample.reshape(bs * ts, d)
    mem_flat = mem_prompt.reshape(bp * n_mem, d)

    pool_p, pool_s, dconv_p, dconv_s, dn_p, dn_s = [], [], [], [], [], []
    win_p = [[] for _ in WIN_CONFIGS]
    win_s = [[] for _ in WIN_CONFIGS]
    memk_p, memv_p, fconv_p, fconv_s = [], [], [], []

    for layer in range(depth):
        if layer % 2 == 0:
            e = layer // 2
            n_in = w_in_ab.shape[2]
            n_pad = -(-n_in // LANES) * LANES
            w_in = jnp.pad(w_in_ab[e], ((0, 0), (0, n_pad - n_in))).astype(BF16)
            w_pool_b = w_pool[e].astype(BF16)
            w_out_b = w_out_ab[e].astype(BF16)
            lane_pad = (DN_HEADS, LANES - 2 * DN_HEADS)
            alog_row = jnp.pad(dn_a_log[e], lane_pad).reshape(1, LANES)
            dtb_row = jnp.pad(dn_dt_bias[e], lane_pad).reshape(1, LANES)
            norm_w = dn_norm_w[e].reshape(1, DN_DIM)
            new_x = []
            for x, b, t, hist_pool, hist_conv, s0, p0 in (
                    (xp, bp, s, jnp.zeros((bp, POOL_HALO, d_pool), F32), jnp.zeros((bp, DN_HALO, 3 * dd), F32),
                     jnp.zeros((bp, DN_HEADS, DN_DIM, DN_DIM), F32), 0),
                    (xs, bs, ts, _pad_rows_top(state_pool[e], POOL_HALO), _pad_rows_top(state_dn_conv[e], DN_HALO),
                     state_dn[e], PAST_LEN)):
                proj, = _norm_proj(x, g_mix[layer], [w_in], [F32], tm=256)
                proj = proj.reshape(b, t, n_pad)
                y_pool = _pool(proj, hist_pool, w_pool_b, pool_scale[e], tm=512, p0=p0)
                o_dn, s_new = _delta(proj, hist_conv, s0, dn_conv_w[e], alog_row, dtb_row, norm_w)
                x = _proj_res([y_pool.reshape(b * t, d_pool), o_dn.reshape(b * t, dd)],
                              [w_out_b[:d_pool], w_out_b[d_pool:]], x, tm=512)
                new_x.append((x, proj, s_new))
            (xp, proj_p, s_p), (xs, proj_s, s_s) = new_x
            pool_p.append(proj_p[:, s - (POOL_HALO - 1):, :d_pool])
            pool_s.append(jnp.concatenate([state_pool[e], proj_s[:, :, :d_pool]], axis=1)[:, -(POOL_HALO - 1):])
            dconv_p.append(proj_p[:, s - (DN_CONV - 1):, d_pool:d_pool + 3 * dd])
            dconv_s.append(jnp.concatenate([state_dn_conv[e], proj_s[:, :, d_pool:d_pool + 3 * dd]],
                                           axis=1)[:, -(DN_CONV - 1):])
            dn_p.append(s_p)
            dn_s.append(s_s)
        else:
            o = layer // 2
            w_qkv = w_qkv_c[o].astype(BF16)
            w_out_b = w_out_c[o].astype(BF16)
            c = w_qkv.shape[1]
            qkv_p = _qkv_rope(xp, g_mix[layer], w_qkv, _rope_tables(jnp.arange(s)), tm=256)
            tab_s = tuple(jnp.tile(u, (bs, 1)) for u in _rope_tables(PAST_LEN + jnp.arange(ts)))
            qkv_s = _qkv_rope(xs, g_mix[layer], w_qkv, tab_s, tm=256)
            qkv_p3 = qkv_p.reshape(bp, s, c)
            qkv_s3 = qkv_s.reshape(bs, ts, c)
            packs = [_swa_group(qkv_p3, gi, dil) for gi, (_, dil) in enumerate(WIN_CONFIGS)]
            xp = _swa_merge(packs, w_out_b, xp, tm=512)
            caches = [ch[o].reshape(bs, ch.shape[2], 2 * D_SWA) for ch in win_caches]
            o_s = _swa_sample(qkv_s3, caches)
            xs = _proj_res([o_s.reshape(bs * ts, D_SWA)], [w_out_b], xs, tm=512)
            for gi, (win, _) in enumerate(WIN_CONFIGS):
                base = gi * 3 * D_SWA
                keep = min(win, s)
                kv_p = qkv_p3[:, s - keep:, base + D_SWA:base + 3 * D_SWA]
                win_p[gi].append(kv_p.reshape(bp, keep, 2, SWA_HEADS, SWA_DIM))
                kv_s = qkv_s3[:, :, base + D_SWA:base + 3 * D_SWA]
                win_s[gi].append(kv_s.reshape(bs, ts, 2, SWA_HEADS, SWA_DIM))

        wq = w_mem_q[layer].astype(BF16)
        wo = w_mem_o[layer].astype(BF16)
        mk, mv = _norm_proj(mem_flat, g_mem_kv[layer], [w_mem_k[layer].astype(BF16), w_mem_v[layer].astype(BF16)],
                            [F32, F32], tm=256)
        hd = d // MEM_HEADS
        memk_p.append(mk.reshape(bp, n_mem, MEM_HEADS, hd))
        memv_p.append(mv.reshape(bp, n_mem, MEM_HEADS, hd))
        new_x = []
        for x, b, t, kk, vv in ((xp, bp, s, mk.reshape(bp, n_mem, d), mv.reshape(bp, n_mem, d)),
                                (xs, bs, ts, cache_mem_k[layer].reshape(bs, n_mem, d),
                                 cache_mem_v[layer].reshape(bs, n_mem, d))):
            q, = _norm_proj(x, g_mem_q[layer], [wq], [BF16], tm=512)
            att = _mem_attn(q.reshape(b, t, d), kk, vv, tm=512)
            new_x.append(_proj_res([att.reshape(b * t, d)], [wo], x, tm=512))
        xp, xs = new_x

        gf = g_final if layer == depth - 1 else None
        wup_b = w_up[layer].astype(BF16)
        wdn_b = w_down[layer].astype(BF16)
        yp, st_p = _ffn(xp.reshape(bp, s, d), g_ffn[layer], wup_b, ffn_conv_w[layer], wdn_b,
                        jnp.zeros((bp, 8, 2 * dff), F32), tm=512, step=1, g_final=gf)
        xp = yp.reshape(bp * s, d)
        fconv_p.append(st_p[:, -2:])
        xs_tm = xs.reshape(bs, ts, d).transpose(1, 0, 2).reshape(1, ts * bs, d)
        hist_tm = state_ffn_conv[layer].transpose(1, 0, 2).reshape(1, 2 * bs, 2 * dff)
        ys, st_s = _ffn(xs_tm, g_ffn[layer], wup_b, ffn_conv_w[layer], wdn_b, hist_tm, tm=ts * bs, step=bs,
                        g_final=gf)
        xs = ys.reshape(ts, bs, d).transpose(1, 0, 2).reshape(bs * ts, d)
        fconv_s.append(st_s.reshape(2, bs, 2 * dff).transpose(1, 0, 2))

    return (xp.reshape(bp, s, d), xs.reshape(bs, ts, d),
            jnp.stack(pool_p), jnp.stack(pool_s),
            jnp.stack(dconv_p), jnp.stack(dconv_s),
            jnp.stack(dn_p), jnp.stack(dn_s),
            jnp.stack(win_p[0]), jnp.stack(win_s[0]),
            jnp.stack(win_p[1]), jnp.stack(win_s[1]),
            jnp.stack(win_p[2]), jnp.stack(win_s[2]),
            jnp.stack(memk_p), jnp.stack(memv_p),
            jnp.stack(fconv_p), jnp.stack(fconv_s))
```

```python
import functools

import jax
import jax.numpy as jnp
from jax import lax
from jax.experimental import pallas as pl
from jax.experimental.pallas import tpu as pltpu

F32 = jnp.float32
BF16 = jnp.bfloat16
EPS = 1e-6

PAST_LEN = 8192
POOL_WINDOWS = (2, 4, 8, 16)
POOL_HALO = 16
DN_HEADS = 4
DN_DIM = 128
DN_CONV = 4
DN_CHUNK = 128
DN_HALO = 8
DN_BATCH = 4
WIN_CONFIGS = ((128, 1), (512, 4), (2048, 16))
SWA_HEADS = 8
SWA_DIM = 64
D_SWA = SWA_HEADS * SWA_DIM
SWA_BLOCK = 128
SWA_BLOCKS_PER_STEP = (8, 2, 1)
ROT_HALF = 8
ROPE_THETA = 500000.0
MEM_HEADS = 4
FFN_CHUNK = 256
LANES = 128
VMEM_LIMIT_BYTES = 56 * 1024 * 1024


def _cparams(*sem):
    return pltpu.CompilerParams(dimension_semantics=sem, vmem_limit_bytes=VMEM_LIMIT_BYTES)


def _rms(x, g):
    return x * lax.rsqrt(jnp.mean(x * x, axis=-1, keepdims=True) + EPS) * g


def _sigmoid(x):
    return 1.0 / (1.0 + jnp.exp(-x))


def _dot(a, b):
    return jnp.dot(a, b, preferred_element_type=F32)


def _dot_nt(a, b):
    return lax.dot_general(a, b, (((1,), (1,)), ((), ())), preferred_element_type=F32)


def _dot_tn(a, b):
    return lax.dot_general(a, b, (((0,), (0,)), ((), ())), preferred_element_type=F32)


def _sel(arr, lead, block=None, index=None):
    nl = len(lead)
    shape = tuple(arr.shape[nl:]) if block is None else tuple(block)
    idx = (0,) * len(shape) if index is None else tuple(index)
    return pl.BlockSpec((None,) * nl + shape, lambda *_: tuple(lead) + idx)


def _norm_proj_kernel(x_ref, g_ref, *refs, n_w, tn):
    w_refs, o_refs = refs[:n_w], refs[n_w:]
    h = _rms(x_ref[...], g_ref[...]).astype(BF16)
    for w_ref, o_ref in zip(w_refs, o_refs):
        n = w_ref.shape[1]
        for j in range(0, n, tn):
            jw = min(tn, n - j)
            o_ref[:, j:j + jw] = _dot(h, w_ref[:, j:j + jw]).astype(o_ref.dtype)


def _norm_proj(x, g, ws, out_dtypes, tm):
    m, d = x.shape
    tm = min(tm, m)
    g_arr, g_lead = g
    in_specs = [pl.BlockSpec((tm, d), lambda i: (i, 0)), _sel(g_arr, g_lead)]
    in_specs += [_sel(w, lead) for w, lead in ws]
    widths = [w.shape[-1] for w, _ in ws]
    return pl.pallas_call(
        functools.partial(_norm_proj_kernel, n_w=len(ws), tn=512),
        grid=(m // tm,),
        in_specs=in_specs,
        out_specs=[pl.BlockSpec((tm, n), lambda i: (i, 0)) for n in widths],
        out_shape=[jax.ShapeDtypeStruct((m, n), dt) for n, dt in zip(widths, out_dtypes)],
        compiler_params=_cparams("parallel"),
        name="norm_proj",
    )(x, g_arr, *[w for w, _ in ws])


def _qkv_rope_kernel(x_ref, g_ref, w_ref, cos_ref, sa_ref, sb_ref, o_ref):
    h = _rms(x_ref[...], g_ref[...]).astype(BF16)
    cos, sa, sb = cos_ref[...], sa_ref[...], sb_ref[...]
    for piece in range(w_ref.shape[1] // D_SWA):
        c0 = piece * D_SWA
        y = _dot(h, w_ref[:, c0:c0 + D_SWA])
        if piece % 3 == 2:
            o_ref[:, c0:c0 + D_SWA] = y
            continue
        for a in range(D_SWA // LANES):
            ya = y[:, a * LANES:(a + 1) * LANES]
            rot = (ya * cos + pltpu.roll(ya, LANES - ROT_HALF, axis=1) * sa
                   + pltpu.roll(ya, ROT_HALF, axis=1) * sb)
            o_ref[:, c0 + a * LANES:c0 + (a + 1) * LANES] = rot


def _rope_tables(pos):
    t = pos.shape[0]
    inv_freq = ROPE_THETA ** (-jnp.arange(ROT_HALF, dtype=F32) / ROT_HALF)
    ang = pos.astype(F32)[:, None] * inv_freq[None, :]
    cos, sin = jnp.cos(ang), jnp.sin(ang)
    rest = SWA_DIM - 2 * ROT_HALF
    c64 = jnp.concatenate([cos, cos, jnp.ones((t, rest), F32)], axis=1)
    a64 = jnp.concatenate([-sin, jnp.zeros((t, SWA_DIM - ROT_HALF), F32)], axis=1)
    b64 = jnp.concatenate([jnp.zeros((t, ROT_HALF), F32), sin, jnp.zeros((t, rest), F32)], axis=1)
    return tuple(jnp.concatenate([u, u], axis=1) for u in (c64, a64, b64))


def _qkv_rope(x, g, w, tables, tm):
    m, d = x.shape
    w_arr, w_lead = w
    g_arr, g_lead = g
    n = w_arr.shape[-1]
    tm = min(tm, m)
    nper = tables[0].shape[0] // tm
    tab_spec = pl.BlockSpec((tm, LANES), lambda i: (i % nper, 0))
    return pl.pallas_call(
        _qkv_rope_kernel,
        grid=(m // tm,),
        in_specs=[pl.BlockSpec((tm, d), lambda i: (i, 0)), _sel(g_arr, g_lead), _sel(w_arr, w_lead),
                  tab_spec, tab_spec, tab_spec],
        out_specs=pl.BlockSpec((tm, n), lambda i: (i, 0)),
        out_shape=jax.ShapeDtypeStruct((m, n), F32),
        compiler_params=_cparams("parallel"),
        name="qkv_rope",
    )(x, g_arr, w_arr, *tables)


def _proj_res_kernel(*refs, n_in):
    a_refs, w_refs = refs[:n_in], refs[n_in:2 * n_in]
    x_ref, o_ref = refs[2 * n_in], refs[2 * n_in + 1]
    acc = x_ref[...]
    for a_ref, w_ref in zip(a_refs, w_refs):
        acc = acc + _dot(a_ref[...], w_ref[...])
    o_ref[...] = acc


def _proj_res(acts, ws, x, tm):
    m, d = x.shape
    tm = min(tm, m)
    in_specs = [pl.BlockSpec((tm, a.shape[1]), lambda i: (i, 0)) for a in acts]
    in_specs += [_sel(w, lead, block=(a.shape[1], d), index=(blk, 0)) for a, (w, lead, blk) in zip(acts, ws)]
    in_specs += [pl.BlockSpec((tm, d), lambda i: (i, 0))]
    return pl.pallas_call(
        functools.partial(_proj_res_kernel, n_in=len(acts)),
        grid=(m // tm,),
        in_specs=in_specs,
        out_specs=pl.BlockSpec((tm, d), lambda i: (i, 0)),
        out_shape=jax.ShapeDtypeStruct((m, d), F32),
        compiler_params=_cparams("parallel"),
        name="proj_res",
    )(*acts, *[w for w, _, _ in ws], x)


def _pool_kernel(a_ref, hist_ref, w_ref, sc_ref, o_ref, buf, *, tm, p0):
    i = pl.program_id(1)

    @pl.when(i == 0)
    def _():
        buf[0:POOL_HALO, :] = hist_ref[0]

    @pl.when(i > 0)
    def _():
        buf[0:POOL_HALO, :] = buf[tm:tm + POOL_HALO, :]

    buf[POOL_HALO:POOL_HALO + tm, :] = a_ref[0]
    pos = p0 + i * tm + lax.broadcasted_iota(jnp.int32, (tm, 1), 0)
    for gi, win in enumerate(POOL_WINDOWS):
        cs = slice(gi * LANES, (gi + 1) * LANES)
        cur = buf[POOL_HALO:POOL_HALO + tm, cs]
        tot = cur
        for j in range(1, win):
            tot = tot + buf[POOL_HALO - j:POOL_HALO - j + tm, cs]
        cnt = jnp.minimum(pos + 1, win).astype(F32)
        z = tot / cnt - cur
        y = _dot(z.astype(BF16), w_ref[gi]) * sc_ref[:, cs]
        o_ref[0, :, cs] = y.astype(o_ref.dtype)


def _pool(proj, hist, w, scale, tm, p0):
    b, t, _ = proj.shape
    dp = len(POOL_WINDOWS) * LANES
    tm = min(tm, t)
    return pl.pallas_call(
        functools.partial(_pool_kernel, tm=tm, p0=p0),
        grid=(b, t // tm),
        in_specs=[pl.BlockSpec((1, tm, dp), lambda bi, i: (bi, i, 0)),
                  pl.BlockSpec((1, POOL_HALO, dp), lambda bi, i: (bi, 0, 0)),
                  _sel(*w), _sel(*scale)],
        out_specs=pl.BlockSpec((1, tm, dp), lambda bi, i: (bi, i, 0)),
        out_shape=jax.ShapeDtypeStruct((b, t, dp), BF16),
        scratch_shapes=[pltpu.VMEM((POOL_HALO + tm, dp), F32)],
        compiler_params=_cparams("parallel", "arbitrary"),
        name="pool_mix",
    )(proj, hist, w[0], scale[0])


def _cumsum_rows(x):
    n = x.shape[0]
    row = lax.broadcasted_iota(jnp.int32, x.shape, 0)
    sh = 1
    while sh < n:
        x = x + jnp.where(row >= sh, pltpu.roll(x, sh, axis=0), 0.0)
        sh *= 2
    return x


def _delta_kernel(q_ref, k_ref, v_ref, gate_ref, ba_ref, hist_ref, s0_ref, cw_ref, alog_ref, dtb_ref,
                  nw_ref, o_ref, s_ref, ext, *, rows, nb):
    c = pl.program_id(1)
    C = DN_CHUNK
    dd = DN_HEADS * DN_DIM

    @pl.when(c == 0)
    def _():
        ext[:, 0:DN_HALO, :] = hist_ref[...]
        s_ref[...] = s0_ref[...]

    ri = lax.broadcasted_iota(jnp.int32, (C, C), 0)
    ci = lax.broadcasted_iota(jnp.int32, (C, C), 1)
    incl = ri >= ci
    strict = ri > ci
    eye = (ri == ci).astype(F32)
    valid = lax.broadcasted_iota(jnp.int32, (C, LANES), 0) < rows

    chains = []
    for bi in range(nb):
        ext[bi, DN_HALO:DN_HALO + rows, 0:dd] = q_ref[bi]
        ext[bi, DN_HALO:DN_HALO + rows, dd:2 * dd] = k_ref[bi]
        ext[bi, DN_HALO:DN_HALO + rows, 2 * dd:3 * dd] = v_ref[bi]
        if rows < C:
            ext[bi, DN_HALO + rows:DN_HALO + C, :] = jnp.zeros((C - rows, 3 * dd), F32)
        conv = cw_ref[DN_CONV - 1:DN_CONV, :] * ext[bi, DN_HALO:DN_HALO + C, :]
        for kk in range(DN_CONV - 1):
            off = DN_HALO - (DN_CONV - 1) + kk
            conv = conv + cw_ref[kk:kk + 1, :] * ext[bi, off:off + C, :]
        act = conv * _sigmoid(conv)
        ext[bi, 0:DN_HALO, :] = ext[bi, rows:rows + DN_HALO, :]

        ba = ba_ref[bi]
        gate = gate_ref[bi]
        if rows < C:
            ba = jnp.concatenate([ba, jnp.zeros((C - rows, LANES), F32)], axis=0)
        beta_t = jnp.where(valid, _sigmoid(ba), 0.0)
        xg = ba + dtb_ref[...]
        softplus = jnp.maximum(xg, 0.0) + jnp.log(1.0 + jnp.exp(-jnp.abs(xg)))
        g_t = jnp.where(valid, -jnp.exp(alog_ref[...]) * softplus, 0.0)
        gcum = _cumsum_rows(g_t)
        gcum_t = gcum.T
        e_g = jnp.exp(gcum)
        g_last = gcum[C - 1:C, :]
        e_rev = jnp.exp(g_last - gcum)
        e_last = jnp.exp(g_last)

        for h in range(DN_HEADS):
            hs = slice(h * DN_DIM, (h + 1) * DN_DIM)
            qh = act[:, hs]
            kh = act[:, dd + h * DN_DIM:dd + (h + 1) * DN_DIM]
            vh = act[:, 2 * dd + h * DN_DIM:2 * dd + (h + 1) * DN_DIM]
            qn = qh * lax.rsqrt(jnp.sum(qh * qh, axis=-1, keepdims=True) + EPS) * (DN_DIM ** -0.5)
            kn = kh * lax.rsqrt(jnp.sum(kh * kh, axis=-1, keepdims=True) + EPS)
            beta = beta_t[:, h:h + 1]
            gcol = gcum[:, DN_HEADS + h:DN_HEADS + h + 1]
            grow = gcum_t[DN_HEADS + h:DN_HEADS + h + 1, :]
            eg = e_g[:, DN_HEADS + h:DN_HEADS + h + 1]
            chains.append(dict(
                bi=bi, h=h, hs=hs, qn=qn, kn=kn, kb=kn.astype(BF16), beta=beta,
                decay=jnp.where(incl, jnp.exp(gcol - grow), 0.0),
                rhs_u=(vh * beta).astype(BF16), rhs_w=(kn * (beta * eg)).astype(BF16),
                qg=(qn * eg).astype(BF16), kg=(kn * e_rev[:, DN_HEADS + h:DN_HEADS + h + 1]).astype(BF16),
                el=e_last[:, DN_HEADS + h:DN_HEADS + h + 1], gate=gate[:, hs]))

    for ch in chains:
        ch["a"] = jnp.where(strict, _dot_nt(ch["kb"], ch["kb"]) * ch["decay"] * ch["beta"], 0.0)
        a_blk = jnp.where((ri >> 4) == (ci >> 4), ch["a"], 0.0)
        ch["p"] = eye - a_blk
        ch["xp"] = a_blk
    for _ in range(3):
        for ch in chains:
            xb = ch["xp"].astype(BF16)
            ch["xp"] = _dot(xb, xb)
        for ch in chains:
            ch["p"] = ch["p"] + _dot(ch["p"].astype(BF16), ch["xp"].astype(BF16))
    sh = 4
    while (1 << sh) < C:
        off_mask = ((ri >> (sh + 1)) == (ci >> (sh + 1))) & ((ri >> sh) != (ci >> sh))
        for ch in chains:
            ch["pb"] = ch["p"].astype(BF16)
            ch["t"] = _dot(ch["pb"], jnp.where(off_mask, ch["a"], 0.0).astype(BF16))
        for ch in chains:
            ch["p"] = ch["p"] - _dot(ch["t"].astype(BF16), ch["pb"])
        sh += 1
    for ch in chains:
        pb = ch["p"].astype(BF16)
        ch["u"] = _dot(pb, ch["rhs_u"])
        ch["w"] = _dot(pb, ch["rhs_w"])
        ch["qk"] = (_dot_nt(ch["qn"].astype(BF16), ch["kb"]) * ch["decay"]).astype(BF16)
    for ch in chains:
        ch["s"] = s_ref[ch["bi"], ch["h"]]
        ch["sb"] = ch["s"].astype(BF16)
        ch["vb"] = (ch["u"] - _dot(ch["w"].astype(BF16), ch["sb"])).astype(BF16)
    for ch in chains:
        ch["o"] = _dot(ch["qg"], ch["sb"]) + _dot(ch["qk"], ch["vb"])
        s_ref[ch["bi"], ch["h"]] = ch["s"] * ch["el"] + _dot_tn(ch["kg"], ch["vb"])
    for ch in chains:
        o = ch["o"]
        o = o * lax.rsqrt(jnp.mean(o * o, axis=-1, keepdims=True) + EPS) * nw_ref[...]
        o = o[0:rows] * (ch["gate"] * _sigmoid(ch["gate"]))
        o_ref[ch["bi"], :, ch["hs"]] = o.astype(o_ref.dtype)


def _delta(proj, hist, s0, conv_w, alog_row, dtb_row, norm_w):
    b, t, _ = proj.shape
    dd = DN_HEADS * DN_DIM
    rows = min(DN_CHUNK, t)
    nb = DN_BATCH
    s_arr, s_lead = s0
    col = lambda j: pl.BlockSpec((nb, rows, dd), lambda bi, c: (bi, c, j))
    st_block = (nb, DN_HEADS, DN_DIM, DN_DIM)
    return pl.pallas_call(
        functools.partial(_delta_kernel, rows=rows, nb=nb),
        grid=(b // nb, t // rows),
        in_specs=[col(1), col(2), col(3), col(4),
                  pl.BlockSpec((nb, rows, LANES), lambda bi, c: (bi, c, 5 * dd // LANES)),
                  pl.BlockSpec((nb, DN_HALO, 3 * dd), lambda bi, c: (bi, 0, 0)),
                  pl.BlockSpec((None,) * len(s_lead) + st_block, lambda bi, c: tuple(s_lead) + (bi, 0, 0, 0)),
                  _sel(*conv_w), _sel(*alog_row), _sel(*dtb_row), _sel(*norm_w)],
        out_specs=[pl.BlockSpec((nb, rows, dd), lambda bi, c: (bi, c, 0)),
                   pl.BlockSpec(st_block, lambda bi, c: (bi, 0, 0, 0))],
        out_shape=[jax.ShapeDtypeStruct((b, t, dd), BF16),
                   jax.ShapeDtypeStruct((b, DN_HEADS, DN_DIM, DN_DIM), F32)],
        scratch_shapes=[pltpu.VMEM((nb, DN_HALO + DN_CHUNK, 3 * dd), F32)],
        compiler_params=_cparams("parallel", "arbitrary"),
        name="delta_rule",
    )(proj, proj, proj, proj, proj, hist, s_arr, conv_w[0], alog_row[0], dtb_row[0], norm_w[0])


def _swa_group_kernel(q_ref, k_ref, v_ref, num_ref, stat_ref, kst, vst, kpv, vpv, *, dil, nblk):
    n = pl.program_id(2)
    blk = SWA_BLOCK
    span = blk * nblk

    def rows_of(r, first_blk, n_blk):
        if dil == 1:
            return pl.ds(first_blk * blk, n_blk * blk)
        return pl.ds(r + dil * blk * first_blk, n_blk * blk, stride=dil)

    @pl.when(n == 0)
    def _():
        kpv[...] = jnp.zeros_like(kpv)
        vpv[...] = jnp.zeros_like(vpv)

    for r in range(dil):
        kst[r] = k_ref[0, rows_of(r, 0, nblk), :].astype(BF16)
        vst[r] = v_ref[0, rows_of(r, 0, nblk), :].astype(BF16)

    qi = lax.broadcasted_iota(jnp.int32, (blk, blk), 0)
    kj = lax.broadcasted_iota(jnp.int32, (blk, blk), 1)
    below = kj <= qi
    above = kj >= qi
    above_first = above & (n > 0)
    lane = lax.broadcasted_iota(jnp.int32, (1, LANES), 1)
    lane32 = lax.broadcasted_iota(jnp.int32, (blk, LANES), 1) >> 5
    zero = jnp.zeros((), BF16)
    scale = SWA_DIM ** -0.5

    bodies = []
    for r in range(dil):
        q_r = (q_ref[0, rows_of(r, 0, nblk), :] * scale).astype(BF16)
        for j in range(nblk):
            rs = slice(j * blk, (j + 1) * blk)
            if j == 0:
                kp, vp, prev_ok = kpv[r], vpv[r], above_first
            else:
                ps = slice((j - 1) * blk, j * blk)
                kp, vp, prev_ok = kst[r, ps, :], vst[r, ps, :], above
            kc, vc = kst[r, rs, :], vst[r, rs, :]
            for half in range(2):
                hm = (lane < SWA_DIM) if half == 0 else (lane >= SWA_DIM)
                qh = jnp.where(hm, q_r[rs], zero)
                bodies.append(dict(
                    r=r, j=j, half=half, hm=hm, vp=vp, vc=vc,
                    s_p=jnp.where(prev_ok, _dot_nt(qh, kp), -jnp.inf),
                    s_c=jnp.where(below, _dot_nt(qh, kc), -jnp.inf)))
    for bd in bodies:
        bd["m"] = jnp.max(jnp.maximum(bd["s_p"], bd["s_c"]), axis=-1, keepdims=True)
        e_p = jnp.exp(bd["s_p"] - bd["m"])
        e_c = jnp.exp(bd["s_c"] - bd["m"])
        bd["den"] = jnp.sum(e_p + e_c, axis=-1, keepdims=True)
        bd["e_p"], bd["e_c"] = e_p.astype(BF16), e_c.astype(BF16)
    for b0, b1 in zip(bodies[0::2], bodies[1::2]):
        acc = None
        stat = jnp.zeros((blk, LANES), F32)
        for bd in (b0, b1):
            part = (_dot(bd["e_p"], jnp.where(bd["hm"], bd["vp"], zero))
                    + _dot(bd["e_c"], jnp.where(bd["hm"], bd["vc"], zero)))
            acc = part if acc is None else acc + part
            stat = jnp.where(lane32 == bd["half"], bd["m"], stat)
            stat = jnp.where(lane32 == bd["half"] + 2, bd["den"], stat)
        num_ref[0, rows_of(b0["r"], b0["j"], 1), :] = acc
        stat_ref[0, rows_of(b0["r"], b0["j"], 1), :] = stat

    for r in range(dil):
        kpv[r] = kst[r, span - blk:span, :]
        vpv[r] = vst[r, span - blk:span, :]


def _swa_group(qkv, gi, dil, nblk):
    b, s, _ = qkv.shape
    rows = SWA_BLOCK * dil * nblk
    npair = D_SWA // LANES
    piece = lambda j: pl.BlockSpec((1, rows, LANES), lambda bi, pp, n: (bi, n, (3 * gi + j) * npair + pp))
    out_spec = pl.BlockSpec((1, rows, LANES), lambda bi, pp, n: (bi, n, pp))
    stage = pltpu.VMEM((dil, SWA_BLOCK * nblk, LANES), BF16)
    prev = pltpu.VMEM((dil, SWA_BLOCK, LANES), BF16)
    num, stat = pl.pallas_call(
        functools.partial(_swa_group_kernel, dil=dil, nblk=nblk),
        grid=(b, npair, s // rows),
        in_specs=[piece(0), piece(1), piece(2)],
        out_specs=[out_spec, out_spec],
        out_shape=[jax.ShapeDtypeStruct((b, s, D_SWA), F32), jax.ShapeDtypeStruct((b, s, D_SWA), F32)],
        scratch_shapes=[stage, stage, prev, prev],
        compiler_params=_cparams("parallel", "parallel", "arbitrary"),
        name="swa_group",
    )(qkv, qkv, qkv)
    return num.reshape(b * s, D_SWA), stat.reshape(b * s, D_SWA)


def _swa_merge_kernel(n0_ref, n1_ref, n2_ref, s0_ref, s1_ref, s2_ref, ex_ref, w_ref, x_ref, o_ref):
    nums, stats = (n0_ref, n1_ref, n2_ref), (s0_ref, s1_ref, s2_ref)
    tm = x_ref.shape[0]
    lane = lax.broadcasted_iota(jnp.int32, (tm, LANES), 1)
    pairs = []
    for pp in range(D_SWA // LANES):
        sl = slice(pp * LANES, (pp + 1) * LANES)
        tiles = [st[:, sl] for st in stats]
        mx = jnp.maximum(jnp.maximum(tiles[0], tiles[1]), tiles[2])
        ws = [jnp.exp(t - mx) for t in tiles]
        dens = [pltpu.roll(t, LANES // 2, axis=1) for t in tiles]
        tot = ws[0] * dens[0] + ws[1] * dens[1] + ws[2] * dens[2]
        o = jnp.zeros((tm, LANES), F32)
        for nr, wg in zip(nums, ws):
            coef = jnp.where(lane < LANES // 2, wg / tot, 0.0)
            coef = jnp.dot(coef, ex_ref[...], preferred_element_type=F32, precision=lax.Precision.HIGHEST)
            o = o + coef * nr[:, sl]
        pairs.append(o.astype(BF16))
    o_ref[...] = x_ref[...] + _dot(jnp.concatenate(pairs, axis=1), w_ref[...])


def _swa_merge(nums, stats, w, x, tm):
    m, d = x.shape
    lane = jnp.arange(LANES)[:, None]
    colh = jnp.arange(LANES)[None, :] // SWA_DIM
    expand = (lane == colh * (LANES // 4)).astype(F32)
    nspec = pl.BlockSpec((tm, D_SWA), lambda i: (i, 0))
    return pl.pallas_call(
        _swa_merge_kernel,
        grid=(m // tm,),
        in_specs=[nspec] * 6 + [pl.BlockSpec(expand.shape, lambda i: (0, 0)),
                  _sel(*w), pl.BlockSpec((tm, d), lambda i: (i, 0))],
        out_specs=pl.BlockSpec((tm, d), lambda i: (i, 0)),
        out_shape=jax.ShapeDtypeStruct((m, d), F32),
        compiler_params=_cparams("parallel"),
        name="swa_merge",
    )(*nums, *stats, expand, w[0], x)


def _swa_sample_kernel(qkv_ref, c0_ref, c1_ref, c2_ref, o_ref, *scr, t_new):
    caches = (c0_ref, c1_ref, c2_ref)
    rows = SWA_HEADS * t_new
    rh = jnp.concatenate([jnp.full((t_new, D_SWA), h, jnp.int32) for h in range(SWA_HEADS)], axis=0)
    chd = lax.broadcasted_iota(jnp.int32, (rows, D_SWA), 1) >> 6
    head_mask = rh == chd
    scores, vals = [], []
    for gi, (win, dil) in enumerate(WIN_CONFIGS):
        kall, vall = scr[2 * gi], scr[2 * gi + 1]
        n_hist = caches[gi].shape[1]
        base = gi * 3 * D_SWA
        q = qkv_ref[0, :, base:base + D_SWA]
        kall[0:n_hist, :] = caches[gi][0, :, 0:D_SWA].astype(BF16)
        vall[0:n_hist, :] = caches[gi][0, :, D_SWA:2 * D_SWA].astype(BF16)
        pad = jnp.zeros((LANES - t_new, D_SWA), BF16)
        kall[n_hist:n_hist + t_new, :] = qkv_ref[0, :, base + D_SWA:base + 2 * D_SWA].astype(BF16)
        vall[n_hist:n_hist + t_new, :] = qkv_ref[0, :, base + 2 * D_SWA:base + 3 * D_SWA].astype(BF16)
        kall[n_hist + t_new:n_hist + LANES, :] = pad
        vall[n_hist + t_new:n_hist + LANES, :] = pad
        qrows = jnp.where(head_mask, jnp.concatenate([q] * SWA_HEADS, axis=0), 0.0).astype(BF16)
        s = _dot_nt(qrows, kall[...]) * (SWA_DIM ** -0.5)
        nk = n_hist + LANES
        tok = jnp.concatenate([lax.broadcasted_iota(jnp.int32, (t_new, nk), 0)] * SWA_HEADS, axis=0)
        idx = lax.broadcasted_iota(jnp.int32, (rows, nk), 1)
        dist = n_hist + tok - idx
        valid = (dist >= 0) & (dist <= win) & ((dist & (dil - 1)) == 0)
        scores.append(jnp.where(valid, s, -jnp.inf))
        vals.append(vall)
    m = scores[0].max(axis=-1, keepdims=True)
    for s in scores[1:]:
        m = jnp.maximum(m, s.max(axis=-1, keepdims=True))
    den = jnp.zeros((rows, 1), F32)
    num = jnp.zeros((rows, D_SWA), F32)
    for s, vall in zip(scores, vals):
        e = jnp.exp(s - m)
        den = den + jnp.sum(e, axis=-1, keepdims=True)
        num = num + _dot(e.astype(BF16), vall[...])
    o = jnp.where(head_mask, num / den, 0.0)
    out = o[0:t_new]
    for h in range(1, SWA_HEADS):
        out = out + o[h * t_new:(h + 1) * t_new]
    o_ref[0] = out.astype(o_ref.dtype)


def _swa_sample(qkv, caches):
    b, t, c = qkv.shape
    scratch = []
    for ch in caches:
        scratch += [pltpu.VMEM((ch.shape[1] + LANES, D_SWA), BF16)] * 2
    return pl.pallas_call(
        functools.partial(_swa_sample_kernel, t_new=t),
        grid=(b,),
        in_specs=[pl.BlockSpec((1, t, c), lambda bi: (bi, 0, 0))]
        + [pl.BlockSpec((1,) + ch.shape[1:], lambda bi: (bi, 0, 0)) for ch in caches],
        out_specs=pl.BlockSpec((1, t, D_SWA), lambda bi: (bi, 0, 0)),
        out_shape=jax.ShapeDtypeStruct((b, t, D_SWA), BF16),
        scratch_shapes=scratch,
        compiler_params=_cparams("arbitrary"),
        name="swa_sample",
    )(qkv, *caches)


def _mem_attn_kernel(q_ref, k_ref, v_ref, o_ref):
    d = q_ref.shape[2]
    hd = d // MEM_HEADS
    for h in range(MEM_HEADS):
        sl = slice(h * hd, (h + 1) * hd)
        s = _dot_nt(q_ref[0, :, sl], k_ref[0, :, sl].astype(BF16)) * (hd ** -0.5)
        e = jnp.exp(s - jnp.max(s, axis=-1, keepdims=True))
        o = _dot(e.astype(BF16), v_ref[0, :, sl].astype(BF16)) / jnp.sum(e, axis=-1, keepdims=True)
        o_ref[0, :, sl] = o.astype(o_ref.dtype)


def _mem_attn(q, k, v, tm):
    b, t, d = q.shape
    n = k.shape[1]
    tm = min(tm, t)
    return pl.pallas_call(
        _mem_attn_kernel,
        grid=(b, t // tm),
        in_specs=[pl.BlockSpec((1, tm, d), lambda bi, i: (bi, i, 0)),
                  pl.BlockSpec((1, n, d), lambda bi, i: (bi, 0, 0)),
                  pl.BlockSpec((1, n, d), lambda bi, i: (bi, 0, 0))],
        out_specs=pl.BlockSpec((1, tm, d), lambda bi, i: (bi, i, 0)),
        out_shape=jax.ShapeDtypeStruct((b, t, d), BF16),
        compiler_params=_cparams("parallel", "parallel"),
        name="mem_attn",
    )(q, k, v)


def _ffn_kernel(x_ref, g_ref, wup_ref, cw_ref, wdn_ref, hist_ref, *rest, tm, halo_rows, step, final):
    if final:
        gf_ref, o_ref, st_ref, u_scr, halo, acc = rest
    else:
        o_ref, st_ref, u_scr, halo, acc = rest
    i = pl.program_id(1)
    dff = wdn_ref.shape[0]
    fc = FFN_CHUNK

    @pl.when(i == 0)
    def _():
        halo[...] = hist_ref[0]

    x = x_ref[0]
    h = _rms(x, g_ref[...]).astype(BF16)
    acc[...] = x
    for c in range(dff // fc):
        conv = []
        for half in range(2):
            cols = slice(half * dff + c * fc, half * dff + (c + 1) * fc)
            u = _dot(h, wup_ref[:, cols])
            u_scr[half, 0:halo_rows, :] = halo[:, cols]
            u_scr[half, halo_rows:halo_rows + tm, :] = u
            o1 = halo_rows - step
            o2 = halo_rows - 2 * step
            conv.append(cw_ref[2:3, cols] * u + cw_ref[1:2, cols] * u_scr[half, o1:o1 + tm, :]
                        + cw_ref[0:1, cols] * u_scr[half, o2:o2 + tm, :])
            halo[:, cols] = u_scr[half, tm:tm + halo_rows, :]
        y = (conv[0] * _sigmoid(conv[0]) * conv[1]).astype(BF16)
        acc[...] += _dot(y, wdn_ref[c * fc:(c + 1) * fc, :])
    out = acc[...]
    if final:
        out = _rms(out, gf_ref[...])
    o_ref[0] = out
    st_ref[0] = halo[...]


def _ffn(x, g, w_up, conv_w, w_down, hist, tm, step, g_final=None):
    b, t, d = x.shape
    dff = w_down[0].shape[-2]
    halo_rows = hist.shape[1]
    tm = min(tm, t)

    def const(arr, lead):
        nl = len(lead)
        shape = tuple(arr.shape[nl:])
        return pl.BlockSpec((None,) * nl + shape, lambda *_: tuple(lead) + (0,) * len(shape),
                            pipeline_mode=pl.Buffered(1))

    in_specs = [pl.BlockSpec((1, tm, d), lambda bi, i: (bi, i, 0)), const(*g), const(*w_up), const(*conv_w),
                const(*w_down), pl.BlockSpec((1, halo_rows, 2 * dff), lambda bi, i: (bi, 0, 0))]
    args = [x, g[0], w_up[0], conv_w[0], w_down[0], hist]
    if g_final is not None:
        in_specs.append(const(g_final, ()))
        args.append(g_final)
    return pl.pallas_call(
        functools.partial(_ffn_kernel, tm=tm, halo_rows=halo_rows, step=step, final=g_final is not None),
        grid=(b, t // tm),
        in_specs=in_specs,
        out_specs=[pl.BlockSpec((1, tm, d), lambda bi, i: (bi, i, 0)),
                   pl.BlockSpec((1, halo_rows, 2 * dff), lambda bi, i: (bi, 0, 0))],
        out_shape=[jax.ShapeDtypeStruct((b, t, d), F32), jax.ShapeDtypeStruct((b, halo_rows, 2 * dff), F32)],
        scratch_shapes=[pltpu.VMEM((2, halo_rows + tm, FFN_CHUNK), F32), pltpu.VMEM((halo_rows, 2 * dff), F32),
                        pltpu.VMEM((tm, d), F32)],
        compiler_params=_cparams("parallel", "arbitrary"),
        name="conv_ffn",
    )(*args)


def _pad_rows_top(a, rows):
    return jnp.pad(a, ((0, 0), (rows - a.shape[1], 0), (0, 0)))


def _row3(a):
    return a.reshape(a.shape[0], 1, a.shape[1])


def kernel(x_prompt, x_sample, state_pool, state_dn_conv, state_dn, cache_win_w128, cache_win_w512, cache_win_w2048, cache_mem_k, cache_mem_v, state_ffn_conv, mem_prompt, g_mix, w_in_ab, w_pool, pool_scale, dn_conv_w, dn_a_log, dn_dt_bias, dn_norm_w, w_out_ab, w_qkv_c, w_out_c, g_mem_q, g_mem_kv, w_mem_q, w_mem_k, w_mem_v, w_mem_o, g_ffn, w_up, ffn_conv_w, w_down, g_final):
    bp, s, d = x_prompt.shape
    bs, ts, _ = x_sample.shape
    depth = g_mix.shape[0]
    n_mem = mem_prompt.shape[1]
    dff = w_down.shape[1]
    d_pool = len(POOL_WINDOWS) * LANES
    dd = DN_HEADS * DN_DIM
    hd = d // MEM_HEADS
    win_caches = (cache_win_w128, cache_win_w512, cache_win_w2048)

    n_in = w_in_ab.shape[2]
    n_pad = -(-n_in // LANES) * LANES
    w_in_b = jnp.pad(w_in_ab, ((0, 0), (0, 0), (0, n_pad - n_in))).astype(BF16)
    w_pool_b, w_out_ab_b = w_pool.astype(BF16), w_out_ab.astype(BF16)
    w_qkv_b, w_out_c_b = w_qkv_c.astype(BF16), w_out_c.astype(BF16)
    wq_b, wk_b, wv_b, wo_b = (w.astype(BF16) for w in (w_mem_q, w_mem_k, w_mem_v, w_mem_o))
    w_up_b, w_down_b = w_up.astype(BF16), w_down.astype(BF16)
    g_mix3, g_mem_q3, g_mem_kv3, g_ffn3 = _row3(g_mix), _row3(g_mem_q), _row3(g_mem_kv), _row3(g_ffn)
    pool_scale3, norm_w3 = _row3(pool_scale), _row3(dn_norm_w)
    lane_pad = ((0, 0), (DN_HEADS, LANES - 2 * DN_HEADS))
    alog3, dtb3 = _row3(jnp.pad(dn_a_log, lane_pad)), _row3(jnp.pad(dn_dt_bias, lane_pad))
    g_final2 = g_final.reshape(1, d)

    xp = x_prompt.reshape(bp * s, d)
    xs = x_sample.reshape(bs * ts, d)
    mem_flat = mem_prompt.reshape(bp * n_mem, d)

    pool_p, pool_s, dconv_p, dconv_s, dn_p, dn_s = [], [], [], [], [], []
    win_p = [[] for _ in WIN_CONFIGS]
    win_s = [[] for _ in WIN_CONFIGS]
    memk_p, memv_p, fconv_p, fconv_s = [], [], [], []

    for layer in range(depth):
        if layer % 2 == 0:
            e = layer // 2
            zero_state = jnp.zeros((1, bp, DN_HEADS, DN_DIM, DN_DIM), F32)
            new_x = []
            for x, b, t, hist_pool, hist_conv, s0, p0 in (
                    (xp, bp, s, jnp.zeros((bp, POOL_HALO, d_pool), F32), jnp.zeros((bp, DN_HALO, 3 * dd), F32),
                     (zero_state, (0,)), 0),
                    (xs, bs, ts, _pad_rows_top(state_pool[e], POOL_HALO), _pad_rows_top(state_dn_conv[e], DN_HALO),
                     (state_dn, (e,)), PAST_LEN)):
                proj, = _norm_proj(x, (g_mix3, (layer,)), [(w_in_b, (e,))], [F32], tm=256)
                proj = proj.reshape(b, t, n_pad)
                y_pool = _pool(proj, hist_pool, (w_pool_b, (e,)), (pool_scale3, (e,)), tm=512, p0=p0)
                o_dn, s_new = _delta(proj, hist_conv, s0, (dn_conv_w, (e,)), (alog3, (e,)), (dtb3, (e,)),
                                     (norm_w3, (e,)))
                x = _proj_res([y_pool.reshape(b * t, d_pool), o_dn.reshape(b * t, dd)],
                              [(w_out_ab_b, (e,), 0), (w_out_ab_b, (e,), 1)], x, tm=512)
                new_x.append((x, proj, s_new))
            (xp, proj_p, s_p), (xs, proj_s, s_s) = new_x
            pool_p.append(proj_p[:, s - (POOL_HALO - 1):, :d_pool])
            pool_s.append(jnp.concatenate([state_pool[e], proj_s[:, :, :d_pool]], axis=1)[:, -(POOL_HALO - 1):])
            dconv_p.append(proj_p[:, s - (DN_CONV - 1):, d_pool:d_pool + 3 * dd])
            dconv_s.append(jnp.concatenate([state_dn_conv[e], proj_s[:, :, d_pool:d_pool + 3 * dd]],
                                           axis=1)[:, -(DN_CONV - 1):])
            dn_p.append(s_p)
            dn_s.append(s_s)
        else:
            o = layer // 2
            c = w_qkv_c.shape[2]
            qkv_p = _qkv_rope(xp, (g_mix3, (layer,)), (w_qkv_b, (o,)), _rope_tables(jnp.arange(s)), tm=256)
            tab_s = tuple(jnp.tile(u, (bs, 1)) for u in _rope_tables(PAST_LEN + jnp.arange(ts)))
            qkv_s = _qkv_rope(xs, (g_mix3, (layer,)), (w_qkv_b, (o,)), tab_s, tm=256)
            qkv_p3 = qkv_p.reshape(bp, s, c)
            qkv_s3 = qkv_s.reshape(bs, ts, c)
            parts = [_swa_group(qkv_p3, gi, dil, SWA_BLOCKS_PER_STEP[gi]) for gi, (_, dil) in enumerate(WIN_CONFIGS)]
            xp = _swa_merge([p[0] for p in parts], [p[1] for p in parts], (w_out_c_b, (o,)), xp, tm=512)
            caches = [ch[o].reshape(bs, ch.shape[2], 2 * D_SWA) for ch in win_caches]
            o_s = _swa_sample(qkv_s3, caches)
            xs = _proj_res([o_s.reshape(bs * ts, D_SWA)], [(w_out_c_b, (o,), 0)], xs, tm=512)
            for gi, (win, _) in enumerate(WIN_CONFIGS):
                base = gi * 3 * D_SWA
                keep = min(win, s)
                kv_p = qkv_p3[:, s - keep:, base + D_SWA:base + 3 * D_SWA]
                win_p[gi].append(kv_p.reshape(bp, keep, 2, SWA_HEADS, SWA_DIM))
                kv_s = qkv_s3[:, :, base + D_SWA:base + 3 * D_SWA]
                win_s[gi].append(kv_s.reshape(bs, ts, 2, SWA_HEADS, SWA_DIM))

        mk, mv = _norm_proj(mem_flat, (g_mem_kv3, (layer,)), [(wk_b, (layer,)), (wv_b, (layer,))], [F32, F32], tm=256)
        memk_p.append(mk.reshape(bp, n_mem, MEM_HEADS, hd))
        memv_p.append(mv.reshape(bp, n_mem, MEM_HEADS, hd))
        new_x = []
        for x, b, t, kk, vv in ((xp, bp, s, mk.reshape(bp, n_mem, d), mv.reshape(bp, n_mem, d)),
                                (xs, bs, ts, cache_mem_k[layer].reshape(bs, n_mem, d),
                                 cache_mem_v[layer].reshape(bs, n_mem, d))):
            q, = _norm_proj(x, (g_mem_q3, (layer,)), [(wq_b, (layer,))], [BF16], tm=512)
            att = _mem_attn(q.reshape(b, t, d), kk, vv, tm=512)
            new_x.append(_proj_res([att.reshape(b * t, d)], [(wo_b, (layer,), 0)], x, tm=512))
        xp, xs = new_x

        gf = g_final2 if layer == depth - 1 else None
        ffn_w = ((g_ffn3, (layer,)), (w_up_b, (layer,)), (ffn_conv_w, (layer,)), (w_down_b, (layer,)))
        yp, st_p = _ffn(xp.reshape(bp, s, d), *ffn_w, jnp.zeros((bp, 8, 2 * dff), F32), tm=512, step=1, g_final=gf)
        xp = yp.reshape(bp * s, d)
        fconv_p.append(st_p[:, -2:])
        xs_tm = xs.reshape(bs, ts, d).transpose(1, 0, 2).reshape(1, ts * bs, d)
        hist_tm = state_ffn_conv[layer].transpose(1, 0, 2).reshape(1, 2 * bs, 2 * dff)
        ys, st_s = _ffn(xs_tm, *ffn_w, hist_tm, tm=ts * bs, step=bs, g_final=gf)
        xs = ys.reshape(ts, bs, d).transpose(1, 0, 2).reshape(bs * ts, d)
        fconv_s.append(st_s.reshape(2, bs, 2 * dff).transpose(1, 0, 2))

    return (xp.reshape(bp, s, d), xs.reshape(bs, ts, d),
            jnp.stack(pool_p), jnp.stack(pool_s),
            jnp.stack(dconv_p), jnp.stack(dconv_s),
            jnp.stack(dn_p), jnp.stack(dn_s),
            jnp.stack(win_p[0]), jnp.stack(win_s[0]),
            jnp.stack(win_p[1]), jnp.stack(win_s[1]),
            jnp.stack(win_p[2]), jnp.stack(win_s[2]),
            jnp.stack(memk_p), jnp.stack(memv_p),
            jnp.stack(fconv_p), jnp.stack(fconv_s))
```

```python
import functools

import jax
import jax.numpy as jnp
from jax import lax
from jax.experimental import pallas as pl
from jax.experimental.pallas import tpu as pltpu

F32 = jnp.float32
BF16 = jnp.bfloat16
EPS = 1e-6

PAST_LEN = 8192
POOL_WINDOWS = (2, 4, 8, 16)
POOL_HALO = 16
DN_HEADS = 4
DN_DIM = 128
DN_CONV = 4
DN_CHUNK = 128
DN_HALO = 8
DN_BATCH = 4
WIN_CONFIGS = ((128, 1), (512, 4), (2048, 16))
SWA_HEADS = 8
SWA_DIM = 64
D_SWA = SWA_HEADS * SWA_DIM
SWA_BLOCK = 128
ROT_HALF = 8
ROPE_THETA = 500000.0
MEM_HEADS = 4
FFN_CHUNK = 256
LANES = 128
VMEM_LIMIT_BYTES = 56 * 1024 * 1024


def _cparams(*sem):
    return pltpu.CompilerParams(dimension_semantics=sem, vmem_limit_bytes=VMEM_LIMIT_BYTES)


def _rms(x, g):
    return x * lax.rsqrt(jnp.mean(x * x, axis=-1, keepdims=True) + EPS) * g


def _sigmoid(x):
    return 1.0 / (1.0 + jnp.exp(-x))


def _dot(a, b):
    return jnp.dot(a, b, preferred_element_type=F32)


def _dot_nt(a, b):
    return lax.dot_general(a, b, (((1,), (1,)), ((), ())), preferred_element_type=F32)


def _dot_tn(a, b):
    return lax.dot_general(a, b, (((0,), (0,)), ((), ())), preferred_element_type=F32)


def _sel(arr, lead, block=None, index=None):
    nl = len(lead)
    shape = tuple(arr.shape[nl:]) if block is None else tuple(block)
    idx = (0,) * len(shape) if index is None else tuple(index)
    return pl.BlockSpec((None,) * nl + shape, lambda *_: tuple(lead) + idx)


def _norm_proj_kernel(x_ref, g_ref, *refs, n_w, tn):
    w_refs, o_refs = refs[:n_w], refs[n_w:]
    h = _rms(x_ref[...], g_ref[...]).astype(BF16)
    for w_ref, o_ref in zip(w_refs, o_refs):
        n = w_ref.shape[1]
        for j in range(0, n, tn):
            jw = min(tn, n - j)
            o_ref[:, j:j + jw] = _dot(h, w_ref[:, j:j + jw]).astype(o_ref.dtype)


def _norm_proj(x, g, ws, out_dtypes, tm):
    m, d = x.shape
    tm = min(tm, m)
    g_arr, g_lead = g
    in_specs = [pl.BlockSpec((tm, d), lambda i: (i, 0)), _sel(g_arr, g_lead)]
    in_specs += [_sel(w, lead) for w, lead in ws]
    widths = [w.shape[-1] for w, _ in ws]
    return pl.pallas_call(
        functools.partial(_norm_proj_kernel, n_w=len(ws), tn=512),
        grid=(m // tm,),
        in_specs=in_specs,
        out_specs=[pl.BlockSpec((tm, n), lambda i: (i, 0)) for n in widths],
        out_shape=[jax.ShapeDtypeStruct((m, n), dt) for n, dt in zip(widths, out_dtypes)],
        compiler_params=_cparams("parallel"),
        name="norm_proj",
    )(x, g_arr, *[w for w, _ in ws])


def _qkv_rope_kernel(x_ref, g_ref, w_ref, cos_ref, sa_ref, sb_ref, o_ref):
    h = _rms(x_ref[...], g_ref[...]).astype(BF16)
    cos, sa, sb = cos_ref[...], sa_ref[...], sb_ref[...]
    for piece in range(w_ref.shape[1] // D_SWA):
        c0 = piece * D_SWA
        y = _dot(h, w_ref[:, c0:c0 + D_SWA])
        if piece % 3 == 2:
            o_ref[:, c0:c0 + D_SWA] = y
            continue
        for a in range(D_SWA // LANES):
            ya = y[:, a * LANES:(a + 1) * LANES]
            rot = (ya * cos + pltpu.roll(ya, LANES - ROT_HALF, axis=1) * sa
                   + pltpu.roll(ya, ROT_HALF, axis=1) * sb)
            o_ref[:, c0 + a * LANES:c0 + (a + 1) * LANES] = rot


def _rope_tables(pos):
    t = pos.shape[0]
    inv_freq = ROPE_THETA ** (-jnp.arange(ROT_HALF, dtype=F32) / ROT_HALF)
    ang = pos.astype(F32)[:, None] * inv_freq[None, :]
    cos, sin = jnp.cos(ang), jnp.sin(ang)
    rest = SWA_DIM - 2 * ROT_HALF
    c64 = jnp.concatenate([cos, cos, jnp.ones((t, rest), F32)], axis=1)
    a64 = jnp.concatenate([-sin, jnp.zeros((t, SWA_DIM - ROT_HALF), F32)], axis=1)
    b64 = jnp.concatenate([jnp.zeros((t, ROT_HALF), F32), sin, jnp.zeros((t, rest), F32)], axis=1)
    return tuple(jnp.concatenate([u, u], axis=1) for u in (c64, a64, b64))


def _qkv_rope(x, g, w, tables, tm):
    m, d = x.shape
    w_arr, w_lead = w
    g_arr, g_lead = g
    n = w_arr.shape[-1]
    tm = min(tm, m)
    nper = tables[0].shape[0] // tm
    tab_spec = pl.BlockSpec((tm, LANES), lambda i: (i % nper, 0))
    return pl.pallas_call(
        _qkv_rope_kernel,
        grid=(m // tm,),
        in_specs=[pl.BlockSpec((tm, d), lambda i: (i, 0)), _sel(g_arr, g_lead), _sel(w_arr, w_lead),
                  tab_spec, tab_spec, tab_spec],
        out_specs=pl.BlockSpec((tm, n), lambda i: (i, 0)),
        out_shape=jax.ShapeDtypeStruct((m, n), F32),
        compiler_params=_cparams("parallel"),
        name="qkv_rope",
    )(x, g_arr, w_arr, *tables)


def _proj_res_kernel(*refs, n_in):
    a_refs, w_refs = refs[:n_in], refs[n_in:2 * n_in]
    x_ref, o_ref = refs[2 * n_in], refs[2 * n_in + 1]
    acc = x_ref[...]
    for a_ref, w_ref in zip(a_refs, w_refs):
        acc = acc + _dot(a_ref[...], w_ref[...])
    o_ref[...] = acc


def _proj_res(acts, ws, x, tm):
    m, d = x.shape
    tm = min(tm, m)
    in_specs = [pl.BlockSpec((tm, a.shape[1]), lambda i: (i, 0)) for a in acts]
    in_specs += [_sel(w, lead, block=(a.shape[1], d), index=(blk, 0)) for a, (w, lead, blk) in zip(acts, ws)]
    in_specs += [pl.BlockSpec((tm, d), lambda i: (i, 0))]
    return pl.pallas_call(
        functools.partial(_proj_res_kernel, n_in=len(acts)),
        grid=(m // tm,),
        in_specs=in_specs,
        out_specs=pl.BlockSpec((tm, d), lambda i: (i, 0)),
        out_shape=jax.ShapeDtypeStruct((m, d), F32),
        compiler_params=_cparams("parallel"),
        name="proj_res",
    )(*acts, *[w for w, _, _ in ws], x)


def _pool_kernel(a_ref, hist_ref, w_ref, sc_ref, o_ref, buf, *, tm, p0):
    i = pl.program_id(1)

    @pl.when(i == 0)
    def _():
        buf[0:POOL_HALO, :] = hist_ref[0]

    @pl.when(i > 0)
    def _():
        buf[0:POOL_HALO, :] = buf[tm:tm + POOL_HALO, :]

    buf[POOL_HALO:POOL_HALO + tm, :] = a_ref[0]
    pos = p0 + i * tm + lax.broadcasted_iota(jnp.int32, (tm, 1), 0)
    for gi, win in enumerate(POOL_WINDOWS):
        cs = slice(gi * LANES, (gi + 1) * LANES)
        cur = buf[POOL_HALO:POOL_HALO + tm, cs]
        tot = cur
        for j in range(1, win):
            tot = tot + buf[POOL_HALO - j:POOL_HALO - j + tm, cs]
        cnt = jnp.minimum(pos + 1, win).astype(F32)
        z = tot / cnt - cur
        y = _dot(z.astype(BF16), w_ref[gi]) * sc_ref[:, cs]
        o_ref[0, :, cs] = y.astype(o_ref.dtype)


def _pool(proj, hist, w, scale, tm, p0):
    b, t, _ = proj.shape
    dp = len(POOL_WINDOWS) * LANES
    tm = min(tm, t)
    return pl.pallas_call(
        functools.partial(_pool_kernel, tm=tm, p0=p0),
        grid=(b, t // tm),
        in_specs=[pl.BlockSpec((1, tm, dp), lambda bi, i: (bi, i, 0)),
                  pl.BlockSpec((1, POOL_HALO, dp), lambda bi, i: (bi, 0, 0)),
                  _sel(*w), _sel(*scale)],
        out_specs=pl.BlockSpec((1, tm, dp), lambda bi, i: (bi, i, 0)),
        out_shape=jax.ShapeDtypeStruct((b, t, dp), BF16),
        scratch_shapes=[pltpu.VMEM((POOL_HALO + tm, dp), F32)],
        compiler_params=_cparams("parallel", "arbitrary"),
        name="pool_mix",
    )(proj, hist, w[0], scale[0])


def _cumsum_rows(x):
    n = x.shape[0]
    row = lax.broadcasted_iota(jnp.int32, x.shape, 0)
    sh = 1
    while sh < n:
        x = x + jnp.where(row >= sh, pltpu.roll(x, sh, axis=0), 0.0)
        sh *= 2
    return x


def _delta_kernel(q_ref, k_ref, v_ref, gate_ref, ba_ref, hist_ref, s0_ref, cw_ref, alog_ref, dtb_ref,
                  nw_ref, o_ref, s_ref, ext, *, rows, nb):
    c = pl.program_id(1)
    C = DN_CHUNK
    dd = DN_HEADS * DN_DIM

    @pl.when(c == 0)
    def _():
        ext[:, 0:DN_HALO, :] = hist_ref[...]
        s_ref[...] = s0_ref[...]

    ri = lax.broadcasted_iota(jnp.int32, (C, C), 0)
    ci = lax.broadcasted_iota(jnp.int32, (C, C), 1)
    incl = ri >= ci
    strict = ri > ci
    eye = (ri == ci).astype(F32)
    valid = lax.broadcasted_iota(jnp.int32, (C, LANES), 0) < rows

    chains = []
    for bi in range(nb):
        ext[bi, DN_HALO:DN_HALO + rows, 0:dd] = q_ref[bi]
        ext[bi, DN_HALO:DN_HALO + rows, dd:2 * dd] = k_ref[bi]
        ext[bi, DN_HALO:DN_HALO + rows, 2 * dd:3 * dd] = v_ref[bi]
        if rows < C:
            ext[bi, DN_HALO + rows:DN_HALO + C, :] = jnp.zeros((C - rows, 3 * dd), F32)
        conv = cw_ref[DN_CONV - 1:DN_CONV, :] * ext[bi, DN_HALO:DN_HALO + C, :]
        for kk in range(DN_CONV - 1):
            off = DN_HALO - (DN_CONV - 1) + kk
            conv = conv + cw_ref[kk:kk + 1, :] * ext[bi, off:off + C, :]
        act = conv * _sigmoid(conv)
        ext[bi, 0:DN_HALO, :] = ext[bi, rows:rows + DN_HALO, :]

        ba = ba_ref[bi]
        gate = gate_ref[bi]
        if rows < C:
            ba = jnp.concatenate([ba, jnp.zeros((C - rows, LANES), F32)], axis=0)
        beta_t = jnp.where(valid, _sigmoid(ba), 0.0)
        xg = ba + dtb_ref[...]
        softplus = jnp.maximum(xg, 0.0) + jnp.log(1.0 + jnp.exp(-jnp.abs(xg)))
        g_t = jnp.where(valid, -jnp.exp(alog_ref[...]) * softplus, 0.0)
        gcum = _cumsum_rows(g_t)
        gcum_t = gcum.T
        e_g = jnp.exp(gcum)
        g_last = gcum[C - 1:C, :]
        e_rev = jnp.exp(g_last - gcum)
        e_last = jnp.exp(g_last)

        for h in range(DN_HEADS):
            hs = slice(h * DN_DIM, (h + 1) * DN_DIM)
            qh = act[:, hs]
            kh = act[:, dd + h * DN_DIM:dd + (h + 1) * DN_DIM]
            vh = act[:, 2 * dd + h * DN_DIM:2 * dd + (h + 1) * DN_DIM]
            qn = qh * lax.rsqrt(jnp.sum(qh * qh, axis=-1, keepdims=True) + EPS) * (DN_DIM ** -0.5)
            kn = kh * lax.rsqrt(jnp.sum(kh * kh, axis=-1, keepdims=True) + EPS)
            beta = beta_t[:, h:h + 1]
            gcol = gcum[:, DN_HEADS + h:DN_HEADS + h + 1]
            grow = gcum_t[DN_HEADS + h:DN_HEADS + h + 1, :]
            eg = e_g[:, DN_HEADS + h:DN_HEADS + h + 1]
            chains.append(dict(
                bi=bi, h=h, hs=hs, qn=qn, kn=kn, kb=kn.astype(BF16), beta=beta,
                decay=jnp.where(incl, jnp.exp(gcol - grow), 0.0),
                rhs_u=(vh * beta).astype(BF16), rhs_w=(kn * (beta * eg)).astype(BF16),
                qg=(qn * eg).astype(BF16), kg=(kn * e_rev[:, DN_HEADS + h:DN_HEADS + h + 1]).astype(BF16),
                el=e_last[:, DN_HEADS + h:DN_HEADS + h + 1], gate=gate[:, hs]))

    for ch in chains:
        ch["a"] = jnp.where(strict, _dot_nt(ch["kb"], ch["kb"]) * ch["decay"] * ch["beta"], 0.0)
        a_blk = jnp.where((ri >> 4) == (ci >> 4), ch["a"], 0.0)
        ch["p"] = eye - a_blk
        ch["xp"] = a_blk
    for _ in range(3):
        for ch in chains:
            xb = ch["xp"].astype(BF16)
            ch["xp"] = _dot(xb, xb)
        for ch in chains:
            ch["p"] = ch["p"] + _dot(ch["p"].astype(BF16), ch["xp"].astype(BF16))
    sh = 4
    while (1 << sh) < C:
        off_mask = ((ri >> (sh + 1)) == (ci >> (sh + 1))) & ((ri >> sh) != (ci >> sh))
        for ch in chains:
            ch["pb"] = ch["p"].astype(BF16)
            ch["t"] = _dot(ch["pb"], jnp.where(off_mask, ch["a"], 0.0).astype(BF16))
        for ch in chains:
            ch["p"] = ch["p"] - _dot(ch["t"].astype(BF16), ch["pb"])
        sh += 1
    for ch in chains:
        pb = ch["p"].astype(BF16)
        ch["u"] = _dot(pb, ch["rhs_u"])
        ch["w"] = _dot(pb, ch["rhs_w"])
        ch["qk"] = (_dot_nt(ch["qn"].astype(BF16), ch["kb"]) * ch["decay"]).astype(BF16)
    for ch in chains:
        ch["s"] = s_ref[ch["bi"], ch["h"]]
        ch["sb"] = ch["s"].astype(BF16)
        ch["vb"] = (ch["u"] - _dot(ch["w"].astype(BF16), ch["sb"])).astype(BF16)
    for ch in chains:
        ch["o"] = _dot(ch["qg"], ch["sb"]) + _dot(ch["qk"], ch["vb"])
        s_ref[ch["bi"], ch["h"]] = ch["s"] * ch["el"] + _dot_tn(ch["kg"], ch["vb"])
    for ch in chains:
        o = ch["o"]
        o = o * lax.rsqrt(jnp.mean(o * o, axis=-1, keepdims=True) + EPS) * nw_ref[...]
        o = o[0:rows] * (ch["gate"] * _sigmoid(ch["gate"]))
        o_ref[ch["bi"], :, ch["hs"]] = o.astype(o_ref.dtype)


def _delta(proj, hist, s0, conv_w, alog_row, dtb_row, norm_w):
    b, t, _ = proj.shape
    dd = DN_HEADS * DN_DIM
    rows = min(DN_CHUNK, t)
    nb = DN_BATCH
    s_arr, s_lead = s0
    col = lambda j: pl.BlockSpec((nb, rows, dd), lambda bi, c: (bi, c, j))
    st_block = (nb, DN_HEADS, DN_DIM, DN_DIM)
    return pl.pallas_call(
        functools.partial(_delta_kernel, rows=rows, nb=nb),
        grid=(b // nb, t // rows),
        in_specs=[col(1), col(2), col(3), col(4),
                  pl.BlockSpec((nb, rows, LANES), lambda bi, c: (bi, c, 5 * dd // LANES)),
                  pl.BlockSpec((nb, DN_HALO, 3 * dd), lambda bi, c: (bi, 0, 0)),
                  pl.BlockSpec((None,) * len(s_lead) + st_block, lambda bi, c: tuple(s_lead) + (bi, 0, 0, 0)),
                  _sel(*conv_w), _sel(*alog_row), _sel(*dtb_row), _sel(*norm_w)],
        out_specs=[pl.BlockSpec((nb, rows, dd), lambda bi, c: (bi, c, 0)),
                   pl.BlockSpec(st_block, lambda bi, c: (bi, 0, 0, 0))],
        out_shape=[jax.ShapeDtypeStruct((b, t, dd), BF16),
                   jax.ShapeDtypeStruct((b, DN_HEADS, DN_DIM, DN_DIM), F32)],
        scratch_shapes=[pltpu.VMEM((nb, DN_HALO + DN_CHUNK, 3 * dd), F32)],
        compiler_params=_cparams("parallel", "arbitrary"),
        name="delta_rule",
    )(proj, proj, proj, proj, proj, hist, s_arr, conv_w[0], alog_row[0], dtb_row[0], norm_w[0])


def _swa_group_bodies(n, q_ref, k_ref, v_ref, num_s, mx_s, den_s, gi, kst, vst, kpv, vpv, dil, nblk):
    blk = SWA_BLOCK
    span = blk * nblk

    def rows_of(r, first_blk, n_blk):
        if dil == 1:
            return pl.ds(first_blk * blk, n_blk * blk)
        return pl.ds(r + dil * blk * first_blk, n_blk * blk, stride=dil)

    @pl.when(n == 0)
    def _():
        kpv[...] = jnp.zeros_like(kpv)
        vpv[...] = jnp.zeros_like(vpv)

    for r in range(dil):
        kst[r] = k_ref[0, rows_of(r, 0, nblk), :].astype(BF16)
        vst[r] = v_ref[0, rows_of(r, 0, nblk), :].astype(BF16)

    qi = lax.broadcasted_iota(jnp.int32, (blk, blk), 0)
    kj = lax.broadcasted_iota(jnp.int32, (blk, blk), 1)
    below = kj <= qi
    above = kj >= qi
    above_first = above & (n > 0)
    lane = lax.broadcasted_iota(jnp.int32, (1, LANES), 1)
    zero = jnp.zeros((), BF16)
    scale = SWA_DIM ** -0.5

    bodies = []
    for r in range(dil):
        q_r = (q_ref[0, rows_of(r, 0, nblk), :] * scale).astype(BF16)
        for j in range(nblk):
            rs = slice(j * blk, (j + 1) * blk)
            if j == 0:
                kp, vp, prev_ok = kpv[r], vpv[r], above_first
            else:
                ps = slice((j - 1) * blk, j * blk)
                kp, vp, prev_ok = kst[r, ps, :], vst[r, ps, :], above
            kc, vc = kst[r, rs, :], vst[r, rs, :]
            for half in range(2):
                hm = (lane < SWA_DIM) if half == 0 else (lane >= SWA_DIM)
                qh = jnp.where(hm, q_r[rs], zero)
                bodies.append(dict(
                    r=r, j=j, half=half, hm=hm, vp=vp, vc=vc,
                    s_p=jnp.where(prev_ok, _dot_nt(qh, kp), -jnp.inf),
                    s_c=jnp.where(below, _dot_nt(qh, kc), -jnp.inf)))
    for bd in bodies:
        bd["m"] = jnp.max(jnp.maximum(bd["s_p"], bd["s_c"]), axis=-1, keepdims=True)
        e_p = jnp.exp(bd["s_p"] - bd["m"])
        e_c = jnp.exp(bd["s_c"] - bd["m"])
        bd["den"] = jnp.sum(e_p + e_c, axis=-1, keepdims=True)
        bd["e_p"], bd["e_c"] = e_p.astype(BF16), e_c.astype(BF16)
    for b0, b1 in zip(bodies[0::2], bodies[1::2]):
        acc = None
        for bd in (b0, b1):
            part = (_dot(bd["e_p"], jnp.where(bd["hm"], bd["vp"], zero))
                    + _dot(bd["e_c"], jnp.where(bd["hm"], bd["vc"], zero)))
            acc = part if acc is None else acc + part
        rows = rows_of(b0["r"], b0["j"], 1)
        num_s[gi, rows, :] = acc
        mx_s[gi, rows, :] = jnp.where(b0["hm"], b0["m"], b1["m"])
        den_s[gi, rows, :] = jnp.where(b0["hm"], b0["den"], b1["den"])

    for r in range(dil):
        kpv[r] = kst[r, span - blk:span, :]
        vpv[r] = vst[r, span - blk:span, :]


def _swa_prompt_kernel(*refs, dils, rows):
    ng = len(dils)
    in_refs, o_ref, scr = refs[:3 * ng], refs[3 * ng], refs[3 * ng + 1:]
    num_s, mx_s, den_s = scr[4 * ng:]
    n = pl.program_id(2)
    for gi, dil in enumerate(dils):
        _swa_group_bodies(n, *in_refs[3 * gi:3 * gi + 3], num_s, mx_s, den_s, gi, *scr[4 * gi:4 * gi + 4],
                          dil, rows // (SWA_BLOCK * dil))
    chunk = 2 * SWA_BLOCK
    for c0 in range(0, rows, chunk):
        rs = slice(c0, c0 + chunk)
        ms = [mx_s[gi, rs, :] for gi in range(ng)]
        mx = functools.reduce(jnp.maximum, ms)
        ws = [jnp.exp(m - mx) for m in ms]
        tot = sum(w * den_s[gi, rs, :] for gi, w in enumerate(ws))
        o = sum((w / tot) * num_s[gi, rs, :] for gi, w in enumerate(ws))
        o_ref[0, rs, :] = o.astype(o_ref.dtype)


def _swa_prompt(qkv):
    b, s, _ = qkv.shape
    dils = tuple(dil for _, dil in WIN_CONFIGS)
    rows = SWA_BLOCK * max(dils)
    npair = D_SWA // LANES
    in_specs, scratch = [], []
    for gi, dil in enumerate(dils):
        for j in range(3):
            in_specs.append(pl.BlockSpec((1, rows, LANES),
                                         lambda bi, pp, n, gi=gi, j=j: (bi, n, (3 * gi + j) * npair + pp)))
        stage = pltpu.VMEM((dil, rows // dil, LANES), BF16)
        prev = pltpu.VMEM((dil, SWA_BLOCK, LANES), BF16)
        scratch += [stage, stage, prev, prev]
    scratch += [pltpu.VMEM((len(dils), rows, LANES), F32)] * 3
    out = pl.pallas_call(
        functools.partial(_swa_prompt_kernel, dils=dils, rows=rows),
        grid=(b, npair, s // rows),
        in_specs=in_specs,
        out_specs=pl.BlockSpec((1, rows, LANES), lambda bi, pp, n: (bi, n, pp)),
        out_shape=jax.ShapeDtypeStruct((b, s, D_SWA), BF16),
        scratch_shapes=scratch,
        compiler_params=_cparams("parallel", "parallel", "arbitrary"),
        name="swa_prompt",
    )(*([qkv] * (3 * len(dils))))
    return out.reshape(b * s, D_SWA)


def _swa_sample_kernel(qkv_ref, c0_ref, c1_ref, c2_ref, o_ref, *, t_new):
    caches = (c0_ref, c1_ref, c2_ref)
    scale = SWA_DIM ** -0.5 * 1.4426950408889634
    tn_idx = lax.broadcasted_iota(jnp.int32, (t_new, SWA_HEADS, 1), 0)
    for t in range(t_new):
        pieces = []
        for gi, (win, dil) in enumerate(WIN_CONFIGS):
            c_ref = caches[gi]
            n_ent = c_ref.shape[0]
            r = t % dil
            q = qkv_ref[0, t, gi, 0] * scale
            l_min = max(0, n_ent - (win - (t - r)) // dil)
            kc, vc = c_ref[l_min:, r, 0], c_ref[l_min:, r, 1]
            s_c = jnp.sum(kc * q, axis=-1, keepdims=True)
            kn, vn = qkv_ref[0, :, gi, 1], qkv_ref[0, :, gi, 2]
            dn = t - tn_idx
            ok_n = (dn >= 0) & (dn <= win) & ((dn & (dil - 1)) == 0)
            s_n = jnp.where(ok_n, jnp.sum(kn * q, axis=-1, keepdims=True), -jnp.inf)
            pieces += [(s_c, vc), (s_n, vn)]
        m = pieces[0][0].max(axis=0, keepdims=True)
        for s, _ in pieces[1:]:
            m = jnp.maximum(m, s.max(axis=0, keepdims=True))
        den = jnp.zeros((1, SWA_HEADS, 1), F32)
        num = jnp.zeros((SWA_HEADS, SWA_DIM), F32)
        for s, v in pieces:
            e = jnp.exp2(s - m)
            den = den + jnp.sum(e, axis=0, keepdims=True)
            num = num + jnp.sum(e * v, axis=0)
        o_ref[0, t] = num / den[0]


def _swa_sample(qkv, caches, layer_idx):
    b, t = qkv.shape[:2]
    cache_specs = []
    for ch, (_, dil) in zip(caches, WIN_CONFIGS):
        used = min(dil, t)
        cache_specs.append(pl.BlockSpec((None, None, ch.shape[2], used) + ch.shape[4:],
                                        lambda bi: (layer_idx, bi, 0, 0, 0, 0, 0)))
    return pl.pallas_call(
        functools.partial(_swa_sample_kernel, t_new=t),
        grid=(b,),
        in_specs=[pl.BlockSpec((1,) + qkv.shape[1:], lambda bi: (bi, 0, 0, 0, 0, 0))] + cache_specs,
        out_specs=pl.BlockSpec((1, t, SWA_HEADS, SWA_DIM), lambda bi: (bi, 0, 0, 0)),
        out_shape=jax.ShapeDtypeStruct((b, t, SWA_HEADS, SWA_DIM), F32),
        compiler_params=_cparams("parallel"),
        name="swa_sample",
    )(qkv, *caches)


def _mem_attn_kernel(q_ref, k_ref, v_ref, o_ref):
    d = q_ref.shape[2]
    hd = d // MEM_HEADS
    for h in range(MEM_HEADS):
        sl = slice(h * hd, (h + 1) * hd)
        s = _dot_nt(q_ref[0, :, sl], k_ref[0, :, sl].astype(BF16)) * (hd ** -0.5)
        e = jnp.exp(s - jnp.max(s, axis=-1, keepdims=True))
        o = _dot(e.astype(BF16), v_ref[0, :, sl].astype(BF16)) / jnp.sum(e, axis=-1, keepdims=True)
        o_ref[0, :, sl] = o.astype(o_ref.dtype)


def _mem_attn(q, k, v, tm):
    b, t, d = q.shape
    n = k.shape[1]
    tm = min(tm, t)
    return pl.pallas_call(
        _mem_attn_kernel,
        grid=(b, t // tm),
        in_specs=[pl.BlockSpec((1, tm, d), lambda bi, i: (bi, i, 0)),
                  pl.BlockSpec((1, n, d), lambda bi, i: (bi, 0, 0)),
                  pl.BlockSpec((1, n, d), lambda bi, i: (bi, 0, 0))],
        out_specs=pl.BlockSpec((1, tm, d), lambda bi, i: (bi, i, 0)),
        out_shape=jax.ShapeDtypeStruct((b, t, d), BF16),
        compiler_params=_cparams("parallel", "parallel"),
        name="mem_attn",
    )(q, k, v)


def _mem_block_kernel(x_ref, g_ref, wq_ref, k_ref, v_ref, wo_ref, o_ref):
    x = x_ref[0]
    d = x.shape[1]
    hd = d // MEM_HEADS
    h = _rms(x, g_ref[...]).astype(BF16)
    heads = [slice(i * hd, (i + 1) * hd) for i in range(MEM_HEADS)]
    qs = [(_dot(h, wq_ref[:, sl]) * (hd ** -0.5)).astype(BF16) for sl in heads]
    ss = [_dot_nt(q, k_ref[0, :, sl].astype(BF16)) for q, sl in zip(qs, heads)]
    es = [jnp.exp(s - jnp.max(s, axis=-1, keepdims=True)) for s in ss]
    outs = [(_dot(e.astype(BF16), v_ref[0, :, sl].astype(BF16)) / jnp.sum(e, axis=-1, keepdims=True)).astype(BF16)
            for e, sl in zip(es, heads)]
    o_ref[0] = x + _dot(jnp.concatenate(outs, axis=1), wo_ref[...])


def _mem_block(x, g, wq, k, v, wo, tm):
    b, t, d = x.shape
    n = k.shape[1]
    return pl.pallas_call(
        _mem_block_kernel,
        grid=(b, t // tm),
        in_specs=[pl.BlockSpec((1, tm, d), lambda bi, i: (bi, i, 0)), _sel(*g), _sel(*wq),
                  pl.BlockSpec((1, n, d), lambda bi, i: (bi, 0, 0)),
                  pl.BlockSpec((1, n, d), lambda bi, i: (bi, 0, 0)), _sel(*wo)],
        out_specs=pl.BlockSpec((1, tm, d), lambda bi, i: (bi, i, 0)),
        out_shape=jax.ShapeDtypeStruct((b, t, d), F32),
        compiler_params=_cparams("parallel", "parallel"),
        name="mem_block",
    )(x, g[0], wq[0], k, v, wo[0])


def _ffn_kernel(x_ref, g_ref, wup_ref, cw_ref, wdn_ref, hist_ref, *rest, tm, halo_rows, step, final):
    if final:
        gf_ref, o_ref, st_ref, u_scr, halo, acc = rest
    else:
        o_ref, st_ref, u_scr, halo, acc = rest
    i = pl.program_id(1)
    dff = wdn_ref.shape[0]
    fc = FFN_CHUNK

    @pl.when(i == 0)
    def _():
        halo[...] = hist_ref[0]

    x = x_ref[0]
    h = _rms(x, g_ref[...]).astype(BF16)
    acc[...] = x
    n_chunks = dff // fc
    o1 = halo_rows - step
    o2 = halo_rows - 2 * step

    def up(c):
        for half in range(2):
            cols = slice(half * dff + c * fc, half * dff + (c + 1) * fc)
            slot = 2 * (c % 2) + half
            u_scr[slot, 0:halo_rows, :] = halo[:, cols]
            u_scr[slot, halo_rows:halo_rows + tm, :] = _dot(h, wup_ref[:, cols])

    up(0)
    for c in range(n_chunks):
        if c + 1 < n_chunks:
            up(c + 1)
        conv = []
        for half in range(2):
            cols = slice(half * dff + c * fc, half * dff + (c + 1) * fc)
            slot = 2 * (c % 2) + half
            conv.append(cw_ref[2:3, cols] * u_scr[slot, halo_rows:halo_rows + tm, :]
                        + cw_ref[1:2, cols] * u_scr[slot, o1:o1 + tm, :]
                        + cw_ref[0:1, cols] * u_scr[slot, o2:o2 + tm, :])
            halo[:, cols] = u_scr[slot, tm:tm + halo_rows, :]
        y = (conv[0] * _sigmoid(conv[0]) * conv[1]).astype(BF16)
        acc[...] += _dot(y, wdn_ref[c * fc:(c + 1) * fc, :])
    out = acc[...]
    if final:
        out = _rms(out, gf_ref[...])
    o_ref[0] = out
    st_ref[0] = halo[...]


def _ffn(x, g, w_up, conv_w, w_down, hist, tm, step, g_final=None):
    b, t, d = x.shape
    dff = w_down[0].shape[-2]
    halo_rows = hist.shape[1]
    tm = min(tm, t)

    def const(arr, lead):
        nl = len(lead)
        shape = tuple(arr.shape[nl:])
        return pl.BlockSpec((None,) * nl + shape, lambda *_: tuple(lead) + (0,) * len(shape),
                            pipeline_mode=pl.Buffered(1))

    in_specs = [pl.BlockSpec((1, tm, d), lambda bi, i: (bi, i, 0)), const(*g), const(*w_up), const(*conv_w),
                const(*w_down), pl.BlockSpec((1, halo_rows, 2 * dff), lambda bi, i: (bi, 0, 0))]
    args = [x, g[0], w_up[0], conv_w[0], w_down[0], hist]
    if g_final is not None:
        in_specs.append(const(g_final, ()))
        args.append(g_final)
    return pl.pallas_call(
        functools.partial(_ffn_kernel, tm=tm, halo_rows=halo_rows, step=step, final=g_final is not None),
        grid=(b, t // tm),
        in_specs=in_specs,
        out_specs=[pl.BlockSpec((1, tm, d), lambda bi, i: (bi, i, 0)),
                   pl.BlockSpec((1, halo_rows, 2 * dff), lambda bi, i: (bi, 0, 0))],
        out_shape=[jax.ShapeDtypeStruct((b, t, d), F32), jax.ShapeDtypeStruct((b, halo_rows, 2 * dff), F32)],
        scratch_shapes=[pltpu.VMEM((4, halo_rows + tm, FFN_CHUNK), F32), pltpu.VMEM((halo_rows, 2 * dff), F32),
                        pltpu.VMEM((tm, d), F32)],
        compiler_params=_cparams("parallel", "arbitrary"),
        name="conv_ffn",
    )(*args)


def _pad_rows_top(a, rows):
    return jnp.pad(a, ((0, 0), (rows - a.shape[1], 0), (0, 0)))


def _row3(a):
    return a.reshape(a.shape[0], 1, a.shape[1])


def kernel(x_prompt, x_sample, state_pool, state_dn_conv, state_dn, cache_win_w128, cache_win_w512, cache_win_w2048, cache_mem_k, cache_mem_v, state_ffn_conv, mem_prompt, g_mix, w_in_ab, w_pool, pool_scale, dn_conv_w, dn_a_log, dn_dt_bias, dn_norm_w, w_out_ab, w_qkv_c, w_out_c, g_mem_q, g_mem_kv, w_mem_q, w_mem_k, w_mem_v, w_mem_o, g_ffn, w_up, ffn_conv_w, w_down, g_final):
    bp, s, d = x_prompt.shape
    bs, ts, _ = x_sample.shape
    depth = g_mix.shape[0]
    n_mem = mem_prompt.shape[1]
    dff = w_down.shape[1]
    d_pool = len(POOL_WINDOWS) * LANES
    dd = DN_HEADS * DN_DIM
    hd = d // MEM_HEADS
    win_caches = (cache_win_w128, cache_win_w512, cache_win_w2048)

    n_in = w_in_ab.shape[2]
    n_pad = -(-n_in // LANES) * LANES
    w_in_b = jnp.pad(w_in_ab, ((0, 0), (0, 0), (0, n_pad - n_in))).astype(BF16)
    w_pool_b, w_out_ab_b = w_pool.astype(BF16), w_out_ab.astype(BF16)
    w_qkv_b, w_out_c_b = w_qkv_c.astype(BF16), w_out_c.astype(BF16)
    wq_b, wk_b, wv_b, wo_b = (w.astype(BF16) for w in (w_mem_q, w_mem_k, w_mem_v, w_mem_o))
    w_up_b, w_down_b = w_up.astype(BF16), w_down.astype(BF16)
    g_mix3, g_mem_q3, g_mem_kv3, g_ffn3 = _row3(g_mix), _row3(g_mem_q), _row3(g_mem_kv), _row3(g_ffn)
    pool_scale3, norm_w3 = _row3(pool_scale), _row3(dn_norm_w)
    lane_pad = ((0, 0), (DN_HEADS, LANES - 2 * DN_HEADS))
    alog3, dtb3 = _row3(jnp.pad(dn_a_log, lane_pad)), _row3(jnp.pad(dn_dt_bias, lane_pad))
    g_final2 = g_final.reshape(1, d)

    xp = x_prompt.reshape(bp * s, d)
    xs = x_sample.reshape(bs * ts, d)
    mem_flat = mem_prompt.reshape(bp * n_mem, d)

    pool_p, pool_s, dconv_p, dconv_s, dn_p, dn_s = [], [], [], [], [], []
    win_p = [[] for _ in WIN_CONFIGS]
    win_s = [[] for _ in WIN_CONFIGS]
    memk_p, memv_p, fconv_p, fconv_s = [], [], [], []

    for layer in range(depth):
        if layer % 2 == 0:
            e = layer // 2
            zero_state = jnp.zeros((1, bp, DN_HEADS, DN_DIM, DN_DIM), F32)
            new_x = []
            for x, b, t, hist_pool, hist_conv, s0, p0 in (
                    (xp, bp, s, jnp.zeros((bp, POOL_HALO, d_pool), F32), jnp.zeros((bp, DN_HALO, 3 * dd), F32),
                     (zero_state, (0,)), 0),
                    (xs, bs, ts, _pad_rows_top(state_pool[e], POOL_HALO), _pad_rows_top(state_dn_conv[e], DN_HALO),
                     (state_dn, (e,)), PAST_LEN)):
                proj, = _norm_proj(x, (g_mix3, (layer,)), [(w_in_b, (e,))], [F32], tm=256)
                proj = proj.reshape(b, t, n_pad)
                y_pool = _pool(proj, hist_pool, (w_pool_b, (e,)), (pool_scale3, (e,)), tm=512, p0=p0)
                o_dn, s_new = _delta(proj, hist_conv, s0, (dn_conv_w, (e,)), (alog3, (e,)), (dtb3, (e,)),
                                     (norm_w3, (e,)))
                x = _proj_res([y_pool.reshape(b * t, d_pool), o_dn.reshape(b * t, dd)],
                              [(w_out_ab_b, (e,), 0), (w_out_ab_b, (e,), 1)], x, tm=512)
                new_x.append((x, proj, s_new))
            (xp, proj_p, s_p), (xs, proj_s, s_s) = new_x
            pool_p.append(proj_p[:, s - (POOL_HALO - 1):, :d_pool])
            pool_s.append(jnp.concatenate([state_pool[e], proj_s[:, :, :d_pool]], axis=1)[:, -(POOL_HALO - 1):])
            dconv_p.append(proj_p[:, s - (DN_CONV - 1):, d_pool:d_pool + 3 * dd])
            dconv_s.append(jnp.concatenate([state_dn_conv[e], proj_s[:, :, d_pool:d_pool + 3 * dd]],
                                           axis=1)[:, -(DN_CONV - 1):])
            dn_p.append(s_p)
            dn_s.append(s_s)
        else:
            o = layer // 2
            c = w_qkv_c.shape[2]
            qkv_p = _qkv_rope(xp, (g_mix3, (layer,)), (w_qkv_b, (o,)), _rope_tables(jnp.arange(s)), tm=256)
            tab_s = tuple(jnp.tile(u, (bs, 1)) for u in _rope_tables(PAST_LEN + jnp.arange(ts)))
            qkv_s = _qkv_rope(xs, (g_mix3, (layer,)), (w_qkv_b, (o,)), tab_s, tm=256)
            qkv_p3 = qkv_p.reshape(bp, s, c)
            qkv_s3 = qkv_s.reshape(bs, ts, c)
            xp = _proj_res([_swa_prompt(qkv_p3)], [(w_out_c_b, (o,), 0)], xp, tm=512)
            caches = [ch.reshape(ch.shape[:2] + (ch.shape[2] // dil, dil) + ch.shape[3:])
                      for ch, (_, dil) in zip(win_caches, WIN_CONFIGS)]
            o_s = _swa_sample(qkv_s.reshape(bs, ts, len(WIN_CONFIGS), 3, SWA_HEADS, SWA_DIM), caches, o)
            xs = _proj_res([o_s.reshape(bs * ts, D_SWA).astype(BF16)], [(w_out_c_b, (o,), 0)], xs, tm=512)
            for gi, (win, _) in enumerate(WIN_CONFIGS):
                base = gi * 3 * D_SWA
                keep = min(win, s)
                kv_p = qkv_p3[:, s - keep:, base + D_SWA:base + 3 * D_SWA]
                win_p[gi].append(kv_p.reshape(bp, keep, 2, SWA_HEADS, SWA_DIM))
                kv_s = qkv_s3[:, :, base + D_SWA:base + 3 * D_SWA]
                win_s[gi].append(kv_s.reshape(bs, ts, 2, SWA_HEADS, SWA_DIM))

        mk, mv = _norm_proj(mem_flat, (g_mem_kv3, (layer,)), [(wk_b, (layer,)), (wv_b, (layer,))], [F32, F32], tm=256)
        memk_p.append(mk.reshape(bp, n_mem, MEM_HEADS, hd))
        memv_p.append(mv.reshape(bp, n_mem, MEM_HEADS, hd))
        xp = _mem_block(xp.reshape(bp, s, d), (g_mem_q3, (layer,)), (wq_b, (layer,)), mk.reshape(bp, n_mem, d),
                        mv.reshape(bp, n_mem, d), (wo_b, (layer,)), tm=512).reshape(bp * s, d)
        q_s, = _norm_proj(xs, (g_mem_q3, (layer,)), [(wq_b, (layer,))], [BF16], tm=512)
        att_s = _mem_attn(q_s.reshape(bs, ts, d), cache_mem_k[layer].reshape(bs, n_mem, d),
                          cache_mem_v[layer].reshape(bs, n_mem, d), tm=ts)
        xs = _proj_res([att_s.reshape(bs * ts, d)], [(wo_b, (layer,), 0)], xs, tm=512)

        gf = g_final2 if layer == depth - 1 else None
        ffn_w = ((g_ffn3, (layer,)), (w_up_b, (layer,)), (ffn_conv_w, (layer,)), (w_down_b, (layer,)))
        yp, st_p = _ffn(xp.reshape(bp, s, d), *ffn_w, jnp.zeros((bp, 8, 2 * dff), F32), tm=512, step=1, g_final=gf)
        xp = yp.reshape(bp * s, d)
        fconv_p.append(st_p[:, -2:])
        xs_tm = xs.reshape(bs, ts, d).transpose(1, 0, 2).reshape(1, ts * bs, d)
        hist_tm = state_ffn_conv[layer].transpose(1, 0, 2).reshape(1, 2 * bs, 2 * dff)
        ys, st_s = _ffn(xs_tm, *ffn_w, hist_tm, tm=ts * bs, step=bs, g_final=gf)
        xs = ys.reshape(ts, bs, d).transpose(1, 0, 2).reshape(bs * ts, d)
        fconv_s.append(st_s.reshape(2, bs, 2 * dff).transpose(1, 0, 2))

    return (xp.reshape(bp, s, d), xs.reshape(bs, ts, d),
            jnp.stack(pool_p), jnp.stack(pool_s),
            jnp.stack(dconv_p), jnp.stack(dconv_s),
            jnp.stack(dn_p), jnp.stack(dn_s),
            jnp.stack(win_p[0]), jnp.stack(win_s[0]),
            jnp.stack(win_p[1]), jnp.stack(win_s[1]),
            jnp.stack(win_p[2]), jnp.stack(win_s[2]),
            jnp.stack(memk_p), jnp.stack(memv_p),
            jnp.stack(fconv_p), jnp.stack(fconv_s))
```

```python
import functools

import jax
import jax.numpy as jnp
from jax import lax
from jax.experimental import pallas as pl
from jax.experimental.pallas import tpu as pltpu

F32 = jnp.float32
BF16 = jnp.bfloat16
EPS = 1e-6

PAST_LEN = 8192
POOL_WINDOWS = (2, 4, 8, 16)
POOL_HALO = 16
DN_HEADS = 4
DN_DIM = 128
DN_CONV = 4
DN_CHUNK = 128
DN_HALO = 8
DN_BATCH = 4
WIN_CONFIGS = ((128, 1), (512, 4), (2048, 16))
SWA_HEADS = 8
SWA_DIM = 64
D_SWA = SWA_HEADS * SWA_DIM
SWA_BLOCK = 128
ROT_HALF = 8
ROPE_THETA = 500000.0
MEM_HEADS = 4
FFN_CHUNK = 256
LANES = 128
VMEM_LIMIT_BYTES = 56 * 1024 * 1024


def _cparams(*sem):
    return pltpu.CompilerParams(dimension_semantics=sem, vmem_limit_bytes=VMEM_LIMIT_BYTES)


def _rms(x, g):
    return x * lax.rsqrt(jnp.mean(x * x, axis=-1, keepdims=True) + EPS) * g


def _sigmoid(x):
    return 1.0 / (1.0 + jnp.exp(-x))


def _dot(a, b):
    return jnp.dot(a, b, preferred_element_type=F32)


def _dot_nt(a, b):
    return lax.dot_general(a, b, (((1,), (1,)), ((), ())), preferred_element_type=F32)


def _dot_tn(a, b):
    return lax.dot_general(a, b, (((0,), (0,)), ((), ())), preferred_element_type=F32)


def _sel(arr, lead, block=None, index=None):
    nl = len(lead)
    shape = tuple(arr.shape[nl:]) if block is None else tuple(block)
    idx = (0,) * len(shape) if index is None else tuple(index)
    return pl.BlockSpec((None,) * nl + shape, lambda *_: tuple(lead) + idx)


def _norm_proj_kernel(x_ref, g_ref, *refs, n_w, tn):
    w_refs, o_refs = refs[:n_w], refs[n_w:]
    h = _rms(x_ref[...], g_ref[...]).astype(BF16)
    for w_ref, o_ref in zip(w_refs, o_refs):
        n = w_ref.shape[1]
        for j in range(0, n, tn):
            jw = min(tn, n - j)
            o_ref[:, j:j + jw] = _dot(h, w_ref[:, j:j + jw]).astype(o_ref.dtype)


def _norm_proj(x, g, ws, out_dtypes, tm):
    m, d = x.shape
    tm = min(tm, m)
    g_arr, g_lead = g
    in_specs = [pl.BlockSpec((tm, d), lambda i: (i, 0)), _sel(g_arr, g_lead)]
    in_specs += [_sel(w, lead) for w, lead in ws]
    widths = [w.shape[-1] for w, _ in ws]
    return pl.pallas_call(
        functools.partial(_norm_proj_kernel, n_w=len(ws), tn=512),
        grid=(m // tm,),
        in_specs=in_specs,
        out_specs=[pl.BlockSpec((tm, n), lambda i: (i, 0)) for n in widths],
        out_shape=[jax.ShapeDtypeStruct((m, n), dt) for n, dt in zip(widths, out_dtypes)],
        compiler_params=_cparams("parallel"),
        name="norm_proj",
    )(x, g_arr, *[w for w, _ in ws])


def _qkv_rope_kernel(x_ref, g_ref, w_ref, cos_ref, sa_ref, sb_ref, o_ref):
    h = _rms(x_ref[...], g_ref[...]).astype(BF16)
    cos, sa, sb = cos_ref[...], sa_ref[...], sb_ref[...]
    for piece in range(w_ref.shape[1] // D_SWA):
        c0 = piece * D_SWA
        y = _dot(h, w_ref[:, c0:c0 + D_SWA])
        if piece % 3 == 2:
            o_ref[:, c0:c0 + D_SWA] = y
            continue
        for a in range(D_SWA // LANES):
            ya = y[:, a * LANES:(a + 1) * LANES]
            rot = (ya * cos + pltpu.roll(ya, LANES - ROT_HALF, axis=1) * sa
                   + pltpu.roll(ya, ROT_HALF, axis=1) * sb)
            o_ref[:, c0 + a * LANES:c0 + (a + 1) * LANES] = rot


def _rope_tables(pos):
    t = pos.shape[0]
    inv_freq = ROPE_THETA ** (-jnp.arange(ROT_HALF, dtype=F32) / ROT_HALF)
    ang = pos.astype(F32)[:, None] * inv_freq[None, :]
    cos, sin = jnp.cos(ang), jnp.sin(ang)
    rest = SWA_DIM - 2 * ROT_HALF
    c64 = jnp.concatenate([cos, cos, jnp.ones((t, rest), F32)], axis=1)
    a64 = jnp.concatenate([-sin, jnp.zeros((t, SWA_DIM - ROT_HALF), F32)], axis=1)
    b64 = jnp.concatenate([jnp.zeros((t, ROT_HALF), F32), sin, jnp.zeros((t, rest), F32)], axis=1)
    return tuple(jnp.concatenate([u, u], axis=1) for u in (c64, a64, b64))


def _qkv_rope(x, g, w, tables, tm):
    m, d = x.shape
    w_arr, w_lead = w
    g_arr, g_lead = g
    n = w_arr.shape[-1]
    tm = min(tm, m)
    nper = tables[0].shape[0] // tm
    tab_spec = pl.BlockSpec((tm, LANES), lambda i: (i % nper, 0))
    return pl.pallas_call(
        _qkv_rope_kernel,
        grid=(m // tm,),
        in_specs=[pl.BlockSpec((tm, d), lambda i: (i, 0)), _sel(g_arr, g_lead), _sel(w_arr, w_lead),
                  tab_spec, tab_spec, tab_spec],
        out_specs=pl.BlockSpec((tm, n), lambda i: (i, 0)),
        out_shape=jax.ShapeDtypeStruct((m, n), F32),
        compiler_params=_cparams("parallel"),
        name="qkv_rope",
    )(x, g_arr, w_arr, *tables)


def _proj_res_kernel(*refs, n_in):
    a_refs, w_refs = refs[:n_in], refs[n_in:2 * n_in]
    x_ref, o_ref = refs[2 * n_in], refs[2 * n_in + 1]
    acc = x_ref[...]
    for a_ref, w_ref in zip(a_refs, w_refs):
        acc = acc + _dot(a_ref[...], w_ref[...])
    o_ref[...] = acc


def _proj_res(acts, ws, x, tm):
    m, d = x.shape
    tm = min(tm, m)
    in_specs = [pl.BlockSpec((tm, a.shape[1]), lambda i: (i, 0)) for a in acts]
    in_specs += [_sel(w, lead, block=(a.shape[1], d), index=(blk, 0)) for a, (w, lead, blk) in zip(acts, ws)]
    in_specs += [pl.BlockSpec((tm, d), lambda i: (i, 0))]
    return pl.pallas_call(
        functools.partial(_proj_res_kernel, n_in=len(acts)),
        grid=(m // tm,),
        in_specs=in_specs,
        out_specs=pl.BlockSpec((tm, d), lambda i: (i, 0)),
        out_shape=jax.ShapeDtypeStruct((m, d), F32),
        compiler_params=_cparams("parallel"),
        name="proj_res",
    )(*acts, *[w for w, _, _ in ws], x)


def _pool_kernel(a_ref, hist_ref, w_ref, sc_ref, o_ref, buf, *, tm, p0):
    i = pl.program_id(1)

    @pl.when(i == 0)
    def _():
        buf[0:POOL_HALO, :] = hist_ref[0]

    @pl.when(i > 0)
    def _():
        buf[0:POOL_HALO, :] = buf[tm:tm + POOL_HALO, :]

    buf[POOL_HALO:POOL_HALO + tm, :] = a_ref[0]
    pos = p0 + i * tm + lax.broadcasted_iota(jnp.int32, (tm, 1), 0)
    for gi, win in enumerate(POOL_WINDOWS):
        cs = slice(gi * LANES, (gi + 1) * LANES)
        cur = buf[POOL_HALO:POOL_HALO + tm, cs]
        tot = cur
        for j in range(1, win):
            tot = tot + buf[POOL_HALO - j:POOL_HALO - j + tm, cs]
        cnt = jnp.minimum(pos + 1, win).astype(F32)
        z = tot / cnt - cur
        y = _dot(z.astype(BF16), w_ref[gi]) * sc_ref[:, cs]
        o_ref[0, :, cs] = y.astype(o_ref.dtype)


def _pool(proj, hist, w, scale, tm, p0):
    b, t, _ = proj.shape
    dp = len(POOL_WINDOWS) * LANES
    tm = min(tm, t)
    return pl.pallas_call(
        functools.partial(_pool_kernel, tm=tm, p0=p0),
        grid=(b, t // tm),
        in_specs=[pl.BlockSpec((1, tm, dp), lambda bi, i: (bi, i, 0)),
                  pl.BlockSpec((1, POOL_HALO, dp), lambda bi, i: (bi, 0, 0)),
                  _sel(*w), _sel(*scale)],
        out_specs=pl.BlockSpec((1, tm, dp), lambda bi, i: (bi, i, 0)),
        out_shape=jax.ShapeDtypeStruct((b, t, dp), BF16),
        scratch_shapes=[pltpu.VMEM((POOL_HALO + tm, dp), F32)],
        compiler_params=_cparams("parallel", "arbitrary"),
        name="pool_mix",
    )(proj, hist, w[0], scale[0])


def _cumsum_rows(x):
    n = x.shape[0]
    row = lax.broadcasted_iota(jnp.int32, x.shape, 0)
    sh = 1
    while sh < n:
        x = x + jnp.where(row >= sh, pltpu.roll(x, sh, axis=0), 0.0)
        sh *= 2
    return x


def _delta_kernel(q_ref, k_ref, v_ref, gate_ref, ba_ref, hist_ref, s0_ref, cw_ref, alog_ref, dtb_ref,
                  nw_ref, o_ref, s_ref, ext, *, rows, nb):
    c = pl.program_id(1)
    C = DN_CHUNK
    dd = DN_HEADS * DN_DIM

    @pl.when(c == 0)
    def _():
        ext[:, 0:DN_HALO, :] = hist_ref[...]
        s_ref[...] = s0_ref[...]

    ri = lax.broadcasted_iota(jnp.int32, (C, C), 0)
    ci = lax.broadcasted_iota(jnp.int32, (C, C), 1)
    incl = ri >= ci
    strict = ri > ci
    eye = (ri == ci).astype(F32)
    valid = lax.broadcasted_iota(jnp.int32, (C, LANES), 0) < rows

    chains = []
    for bi in range(nb):
        ext[bi, DN_HALO:DN_HALO + rows, 0:dd] = q_ref[bi]
        ext[bi, DN_HALO:DN_HALO + rows, dd:2 * dd] = k_ref[bi]
        ext[bi, DN_HALO:DN_HALO + rows, 2 * dd:3 * dd] = v_ref[bi]
        if rows < C:
            ext[bi, DN_HALO + rows:DN_HALO + C, :] = jnp.zeros((C - rows, 3 * dd), F32)
        conv = cw_ref[DN_CONV - 1:DN_CONV, :] * ext[bi, DN_HALO:DN_HALO + C, :]
        for kk in range(DN_CONV - 1):
            off = DN_HALO - (DN_CONV - 1) + kk
            conv = conv + cw_ref[kk:kk + 1, :] * ext[bi, off:off + C, :]
        act = conv * _sigmoid(conv)
        ext[bi, 0:DN_HALO, :] = ext[bi, rows:rows + DN_HALO, :]

        ba = ba_ref[bi]
        gate = gate_ref[bi]
        if rows < C:
            ba = jnp.concatenate([ba, jnp.zeros((C - rows, LANES), F32)], axis=0)
        beta_t = jnp.where(valid, _sigmoid(ba), 0.0)
        xg = ba + dtb_ref[...]
        softplus = jnp.maximum(xg, 0.0) + jnp.log(1.0 + jnp.exp(-jnp.abs(xg)))
        g_t = jnp.where(valid, -jnp.exp(alog_ref[...]) * softplus, 0.0)
        gcum = _cumsum_rows(g_t)
        gcum_t = gcum.T
        e_g = jnp.exp(gcum)
        g_last = gcum[C - 1:C, :]
        e_rev = jnp.exp(g_last - gcum)
        e_last = jnp.exp(g_last)

        for h in range(DN_HEADS):
            hs = slice(h * DN_DIM, (h + 1) * DN_DIM)
            qh = act[:, hs]
            kh = act[:, dd + h * DN_DIM:dd + (h + 1) * DN_DIM]
            vh = act[:, 2 * dd + h * DN_DIM:2 * dd + (h + 1) * DN_DIM]
            qn = qh * lax.rsqrt(jnp.sum(qh * qh, axis=-1, keepdims=True) + EPS) * (DN_DIM ** -0.5)
            kn = kh * lax.rsqrt(jnp.sum(kh * kh, axis=-1, keepdims=True) + EPS)
            beta = beta_t[:, h:h + 1]
            gcol = gcum[:, DN_HEADS + h:DN_HEADS + h + 1]
            grow = gcum_t[DN_HEADS + h:DN_HEADS + h + 1, :]
            eg = e_g[:, DN_HEADS + h:DN_HEADS + h + 1]
            chains.append(dict(
                bi=bi, h=h, hs=hs, qn=qn, kn=kn, kb=kn.astype(BF16), beta=beta,
                decay=jnp.where(incl, jnp.exp(gcol - grow), 0.0),
                rhs_u=(vh * beta).astype(BF16), rhs_w=(kn * (beta * eg)).astype(BF16),
                qg=(qn * eg).astype(BF16), kg=(kn * e_rev[:, DN_HEADS + h:DN_HEADS + h + 1]).astype(BF16),
                el=e_last[:, DN_HEADS + h:DN_HEADS + h + 1], gate=gate[:, hs]))

    for ch in chains:
        ch["a"] = jnp.where(strict, _dot_nt(ch["kb"], ch["kb"]) * ch["decay"] * ch["beta"], 0.0)
        a_blk = jnp.where((ri >> 4) == (ci >> 4), ch["a"], 0.0)
        ch["p"] = eye - a_blk
        ch["xp"] = a_blk
    for _ in range(3):
        for ch in chains:
            xb = ch["xp"].astype(BF16)
            ch["xp"] = _dot(xb, xb)
        for ch in chains:
            ch["p"] = ch["p"] + _dot(ch["p"].astype(BF16), ch["xp"].astype(BF16))
    sh = 4
    while (1 << sh) < C:
        off_mask = ((ri >> (sh + 1)) == (ci >> (sh + 1))) & ((ri >> sh) != (ci >> sh))
        for ch in chains:
            ch["pb"] = ch["p"].astype(BF16)
            ch["t"] = _dot(ch["pb"], jnp.where(off_mask, ch["a"], 0.0).astype(BF16))
        for ch in chains:
            ch["p"] = ch["p"] - _dot(ch["t"].astype(BF16), ch["pb"])
        sh += 1
    for ch in chains:
        pb = ch["p"].astype(BF16)
        ch["u"] = _dot(pb, ch["rhs_u"])
        ch["w"] = _dot(pb, ch["rhs_w"])
        ch["qk"] = (_dot_nt(ch["qn"].astype(BF16), ch["kb"]) * ch["decay"]).astype(BF16)
    for ch in chains:
        ch["s"] = s_ref[ch["bi"], ch["h"]]
        ch["sb"] = ch["s"].astype(BF16)
        ch["vb"] = (ch["u"] - _dot(ch["w"].astype(BF16), ch["sb"])).astype(BF16)
    for ch in chains:
        ch["o"] = _dot(ch["qg"], ch["sb"]) + _dot(ch["qk"], ch["vb"])
        s_ref[ch["bi"], ch["h"]] = ch["s"] * ch["el"] + _dot_tn(ch["kg"], ch["vb"])
    for ch in chains:
        o = ch["o"]
        o = o * lax.rsqrt(jnp.mean(o * o, axis=-1, keepdims=True) + EPS) * nw_ref[...]
        o = o[0:rows] * (ch["gate"] * _sigmoid(ch["gate"]))
        o_ref[ch["bi"], :, ch["hs"]] = o.astype(o_ref.dtype)


def _delta(proj, hist, s0, conv_w, alog_row, dtb_row, norm_w):
    b, t, _ = proj.shape
    dd = DN_HEADS * DN_DIM
    rows = min(DN_CHUNK, t)
    nb = DN_BATCH
    s_arr, s_lead = s0
    col = lambda j: pl.BlockSpec((nb, rows, dd), lambda bi, c: (bi, c, j))
    st_block = (nb, DN_HEADS, DN_DIM, DN_DIM)
    return pl.pallas_call(
        functools.partial(_delta_kernel, rows=rows, nb=nb),
        grid=(b // nb, t // rows),
        in_specs=[col(1), col(2), col(3), col(4),
                  pl.BlockSpec((nb, rows, LANES), lambda bi, c: (bi, c, 5 * dd // LANES)),
                  pl.BlockSpec((nb, DN_HALO, 3 * dd), lambda bi, c: (bi, 0, 0)),
                  pl.BlockSpec((None,) * len(s_lead) + st_block, lambda bi, c: tuple(s_lead) + (bi, 0, 0, 0)),
                  _sel(*conv_w), _sel(*alog_row), _sel(*dtb_row), _sel(*norm_w)],
        out_specs=[pl.BlockSpec((nb, rows, dd), lambda bi, c: (bi, c, 0)),
                   pl.BlockSpec(st_block, lambda bi, c: (bi, 0, 0, 0))],
        out_shape=[jax.ShapeDtypeStruct((b, t, dd), BF16),
                   jax.ShapeDtypeStruct((b, DN_HEADS, DN_DIM, DN_DIM), F32)],
        scratch_shapes=[pltpu.VMEM((nb, DN_HALO + DN_CHUNK, 3 * dd), F32)],
        compiler_params=_cparams("parallel", "arbitrary"),
        name="delta_rule",
    )(proj, proj, proj, proj, proj, hist, s_arr, conv_w[0], alog_row[0], dtb_row[0], norm_w[0])


def _swa_group_bodies(n, q_ref, k_ref, v_ref, num_s, mx_s, den_s, gi, kst, vst, kpv, vpv, dil, nblk):
    blk = SWA_BLOCK
    span = blk * nblk

    def rows_of(r, first_blk, n_blk):
        if dil == 1:
            return pl.ds(first_blk * blk, n_blk * blk)
        return pl.ds(r + dil * blk * first_blk, n_blk * blk, stride=dil)

    @pl.when(n == 0)
    def _():
        kpv[...] = jnp.zeros_like(kpv)
        vpv[...] = jnp.zeros_like(vpv)

    for r in range(dil):
        kst[r] = k_ref[0, rows_of(r, 0, nblk), :].astype(BF16)
        vst[r] = v_ref[0, rows_of(r, 0, nblk), :].astype(BF16)

    qi = lax.broadcasted_iota(jnp.int32, (blk, blk), 0)
    kj = lax.broadcasted_iota(jnp.int32, (blk, blk), 1)
    below = kj <= qi
    above = kj >= qi
    above_first = above & (n > 0)
    lane = lax.broadcasted_iota(jnp.int32, (1, LANES), 1)
    zero = jnp.zeros((), BF16)
    scale = SWA_DIM ** -0.5

    bodies = []
    for r in range(dil):
        q_r = (q_ref[0, rows_of(r, 0, nblk), :] * scale).astype(BF16)
        for j in range(nblk):
            rs = slice(j * blk, (j + 1) * blk)
            if j == 0:
                kp, vp, prev_ok = kpv[r], vpv[r], above_first
            else:
                ps = slice((j - 1) * blk, j * blk)
                kp, vp, prev_ok = kst[r, ps, :], vst[r, ps, :], above
            kc, vc = kst[r, rs, :], vst[r, rs, :]
            for half in range(2):
                hm = (lane < SWA_DIM) if half == 0 else (lane >= SWA_DIM)
                qh = jnp.where(hm, q_r[rs], zero)
                bodies.append(dict(
                    r=r, j=j, half=half, hm=hm, vp=vp, vc=vc,
                    s_p=jnp.where(prev_ok, _dot_nt(qh, kp), -jnp.inf),
                    s_c=jnp.where(below, _dot_nt(qh, kc), -jnp.inf)))
    for bd in bodies:
        bd["m"] = jnp.max(jnp.maximum(bd["s_p"], bd["s_c"]), axis=-1, keepdims=True)
        e_p = jnp.exp(bd["s_p"] - bd["m"])
        e_c = jnp.exp(bd["s_c"] - bd["m"])
        bd["den"] = jnp.sum(e_p + e_c, axis=-1, keepdims=True)
        bd["e_p"], bd["e_c"] = e_p.astype(BF16), e_c.astype(BF16)
    for b0, b1 in zip(bodies[0::2], bodies[1::2]):
        acc = None
        for bd in (b0, b1):
            part = (_dot(bd["e_p"], jnp.where(bd["hm"], bd["vp"], zero))
                    + _dot(bd["e_c"], jnp.where(bd["hm"], bd["vc"], zero)))
            acc = part if acc is None else acc + part
        rows = rows_of(b0["r"], b0["j"], 1)
        num_s[gi, rows, :] = acc
        mx_s[gi, rows, :] = jnp.where(b0["hm"], b0["m"], b1["m"])
        den_s[gi, rows, :] = jnp.where(b0["hm"], b0["den"], b1["den"])

    for r in range(dil):
        kpv[r] = kst[r, span - blk:span, :]
        vpv[r] = vst[r, span - blk:span, :]


def _swa_prompt_kernel(*refs, dils, rows):
    ng = len(dils)
    in_refs, o_ref, scr = refs[:3 * ng], refs[3 * ng], refs[3 * ng + 1:]
    num_s, mx_s, den_s = scr[4 * ng:]
    n = pl.program_id(2)
    for gi, dil in enumerate(dils):
        _swa_group_bodies(n, *in_refs[3 * gi:3 * gi + 3], num_s, mx_s, den_s, gi, *scr[4 * gi:4 * gi + 4],
                          dil, rows // (SWA_BLOCK * dil))
    chunk = 2 * SWA_BLOCK
    for c0 in range(0, rows, chunk):
        rs = slice(c0, c0 + chunk)
        ms = [mx_s[gi, rs, :] for gi in range(ng)]
        mx = functools.reduce(jnp.maximum, ms)
        ws = [jnp.exp(m - mx) for m in ms]
        tot = sum(w * den_s[gi, rs, :] for gi, w in enumerate(ws))
        o = sum((w / tot) * num_s[gi, rs, :] for gi, w in enumerate(ws))
        o_ref[0, rs, :] = o.astype(o_ref.dtype)


def _swa_prompt(qkv):
    b, s, _ = qkv.shape
    dils = tuple(dil for _, dil in WIN_CONFIGS)
    rows = SWA_BLOCK * max(dils)
    npair = D_SWA // LANES
    in_specs, scratch = [], []
    for gi, dil in enumerate(dils):
        for j in range(3):
            in_specs.append(pl.BlockSpec((1, rows, LANES),
                                         lambda bi, pp, n, gi=gi, j=j: (bi, n, (3 * gi + j) * npair + pp)))
        stage = pltpu.VMEM((dil, rows // dil, LANES), BF16)
        prev = pltpu.VMEM((dil, SWA_BLOCK, LANES), BF16)
        scratch += [stage, stage, prev, prev]
    scratch += [pltpu.VMEM((len(dils), rows, LANES), F32)] * 3
    out = pl.pallas_call(
        functools.partial(_swa_prompt_kernel, dils=dils, rows=rows),
        grid=(b, npair, s // rows),
        in_specs=in_specs,
        out_specs=pl.BlockSpec((1, rows, LANES), lambda bi, pp, n: (bi, n, pp)),
        out_shape=jax.ShapeDtypeStruct((b, s, D_SWA), BF16),
        scratch_shapes=scratch,
        compiler_params=_cparams("parallel", "parallel", "arbitrary"),
        name="swa_prompt",
    )(*([qkv] * (3 * len(dils))))
    return out.reshape(b * s, D_SWA)


def _swa_sample_kernel(qkv_ref, c0_ref, c1_ref, c2_ref, o_ref, *, t_new):
    caches = (c0_ref, c1_ref, c2_ref)
    rows = SWA_HEADS * t_new
    rh = jnp.concatenate([jnp.full((t_new, D_SWA), h, jnp.int32) for h in range(SWA_HEADS)], axis=0)
    chd = lax.broadcasted_iota(jnp.int32, (rows, D_SWA), 1) >> 6
    head_mask = rh == chd
    scale = SWA_DIM ** -0.5
    pad = jnp.zeros((LANES - t_new, D_SWA), BF16)
    pieces = []
    for gi, (win, dil) in enumerate(WIN_CONFIGS):
        c_ref = caches[gi]
        n_hist = c_ref.shape[2]
        base = gi * 3 * D_SWA
        q = qkv_ref[0, :, base:base + D_SWA] * scale
        qrows = jnp.where(head_mask, jnp.concatenate([q] * SWA_HEADS, axis=0), 0.0).astype(BF16)
        tok = jnp.concatenate([lax.broadcasted_iota(jnp.int32, (t_new, n_hist), 0)] * SWA_HEADS, axis=0)
        dist = n_hist + tok - lax.broadcasted_iota(jnp.int32, (rows, n_hist), 1)
        ok = (dist <= win) & ((dist & (dil - 1)) == 0)
        s_c = jnp.where(ok, _dot(qrows, c_ref[0].astype(BF16)), -jnp.inf)
        k_new = jnp.concatenate([qkv_ref[0, :, base + D_SWA:base + 2 * D_SWA].astype(BF16), pad], axis=0)
        v_new = jnp.concatenate([qkv_ref[0, :, base + 2 * D_SWA:base + 3 * D_SWA].astype(BF16), pad], axis=0)
        tok_n = jnp.concatenate([lax.broadcasted_iota(jnp.int32, (t_new, LANES), 0)] * SWA_HEADS, axis=0)
        dn = tok_n - lax.broadcasted_iota(jnp.int32, (rows, LANES), 1)
        ok_n = (dn >= 0) & (dn <= win) & ((dn & (dil - 1)) == 0)
        s_n = jnp.where(ok_n, _dot_nt(qrows, k_new), -jnp.inf)
        pieces.append((s_c, s_n, c_ref, v_new))
    m = None
    for s_c, s_n, _, _ in pieces:
        mm = jnp.maximum(jnp.max(s_c, axis=-1, keepdims=True), jnp.max(s_n, axis=-1, keepdims=True))
        m = mm if m is None else jnp.maximum(m, mm)
    den = jnp.zeros((rows, 1), F32)
    num = jnp.zeros((rows, D_SWA), F32)
    for s_c, s_n, c_ref, v_new in pieces:
        e_c = jnp.exp(s_c - m)
        e_n = jnp.exp(s_n - m)
        den = den + jnp.sum(e_c, axis=-1, keepdims=True) + jnp.sum(e_n, axis=-1, keepdims=True)
        num = num + _dot_nt(e_c.astype(BF16), c_ref[1].astype(BF16)) + _dot(e_n.astype(BF16), v_new)
    o = jnp.where(head_mask, num / den, 0.0)
    out = o[0:t_new]
    for h in range(1, SWA_HEADS):
        out = out + o[h * t_new:(h + 1) * t_new]
    o_ref[0] = out.astype(o_ref.dtype)


def _swa_sample(qkv, caches, layer_idx):
    b, t, c = qkv.shape
    return pl.pallas_call(
        functools.partial(_swa_sample_kernel, t_new=t),
        grid=(b,),
        in_specs=[pl.BlockSpec((1, t, c), lambda bi: (bi, 0, 0))]
        + [pl.BlockSpec((None, None) + ch.shape[2:], lambda bi: (layer_idx, bi, 0, 0, 0)) for ch in caches],
        out_specs=pl.BlockSpec((1, t, D_SWA), lambda bi: (bi, 0, 0)),
        out_shape=jax.ShapeDtypeStruct((b, t, D_SWA), BF16),
        compiler_params=_cparams("parallel"),
        name="swa_sample",
    )(qkv, *caches)


def _mem_attn_kernel(q_ref, k_ref, v_ref, o_ref):
    n, nh, hd = k_ref.shape
    t = q_ref.shape[1]
    q = q_ref[0]
    qs = jnp.concatenate([q[:, h * hd:(h + 1) * hd] for h in range(nh)], axis=0)
    k_all = k_ref[...].reshape(n * nh, hd).astype(BF16)
    v_all = v_ref[...].reshape(n * nh, hd).astype(BF16)
    row_head = jnp.concatenate([jnp.full((t, n * nh), h, jnp.int32) for h in range(nh)], axis=0)
    col_head = lax.broadcasted_iota(jnp.int32, (nh * t, n * nh), 1) & (nh - 1)
    s = jnp.where(row_head == col_head, _dot_nt(qs, k_all) * (hd ** -0.5), -jnp.inf)
    e = jnp.exp(s - jnp.max(s, axis=-1, keepdims=True))
    o = _dot(e.astype(BF16), v_all) / jnp.sum(e, axis=-1, keepdims=True)
    o_ref[0] = jnp.concatenate([o[h * t:(h + 1) * t] for h in range(nh)], axis=1).astype(o_ref.dtype)


def _mem_attn(q, k, v, layer_idx):
    b, t, d = q.shape
    kv_spec = pl.BlockSpec((None, None) + k.shape[2:], lambda bi: (layer_idx, bi, 0, 0, 0))
    return pl.pallas_call(
        _mem_attn_kernel,
        grid=(b,),
        in_specs=[pl.BlockSpec((1, t, d), lambda bi: (bi, 0, 0)), kv_spec, kv_spec],
        out_specs=pl.BlockSpec((1, t, d), lambda bi: (bi, 0, 0)),
        out_shape=jax.ShapeDtypeStruct((b, t, d), BF16),
        compiler_params=_cparams("parallel"),
        name="mem_attn",
    )(q, k, v)


def _mem_block_kernel(x_ref, g_ref, wq_ref, k_ref, v_ref, wo_ref, o_ref):
    x = x_ref[0]
    d = x.shape[1]
    hd = d // MEM_HEADS
    h = _rms(x, g_ref[...]).astype(BF16)
    heads = [slice(i * hd, (i + 1) * hd) for i in range(MEM_HEADS)]
    qs = [(_dot(h, wq_ref[:, sl]) * (hd ** -0.5)).astype(BF16) for sl in heads]
    ss = [_dot_nt(q, k_ref[0, :, sl].astype(BF16)) for q, sl in zip(qs, heads)]
    es = [jnp.exp(s - jnp.max(s, axis=-1, keepdims=True)) for s in ss]
    outs = [(_dot(e.astype(BF16), v_ref[0, :, sl].astype(BF16)) / jnp.sum(e, axis=-1, keepdims=True)).astype(BF16)
            for e, sl in zip(es, heads)]
    o_ref[0] = x + _dot(jnp.concatenate(outs, axis=1), wo_ref[...])


def _mem_block(x, g, wq, k, v, wo, tm):
    b, t, d = x.shape
    n = k.shape[1]
    return pl.pallas_call(
        _mem_block_kernel,
        grid=(b, t // tm),
        in_specs=[pl.BlockSpec((1, tm, d), lambda bi, i: (bi, i, 0)), _sel(*g), _sel(*wq),
                  pl.BlockSpec((1, n, d), lambda bi, i: (bi, 0, 0)),
                  pl.BlockSpec((1, n, d), lambda bi, i: (bi, 0, 0)), _sel(*wo)],
        out_specs=pl.BlockSpec((1, tm, d), lambda bi, i: (bi, i, 0)),
        out_shape=jax.ShapeDtypeStruct((b, t, d), F32),
        compiler_params=_cparams("parallel", "parallel"),
        name="mem_block",
    )(x, g[0], wq[0], k, v, wo[0])


def _ffn_kernel(x_ref, g_ref, wup_ref, cw_ref, wdn_ref, hist_ref, *rest, tm, halo_rows, step, final):
    if final:
        gf_ref, o_ref, st_ref, u_scr, halo, acc = rest
    else:
        o_ref, st_ref, u_scr, halo, acc = rest
    i = pl.program_id(1)
    dff = wdn_ref.shape[0]
    fc = FFN_CHUNK

    @pl.when(i == 0)
    def _():
        halo[...] = hist_ref[0]

    x = x_ref[0]
    h = _rms(x, g_ref[...]).astype(BF16)
    acc[...] = x
    n_chunks = dff // fc
    o1 = halo_rows - step
    o2 = halo_rows - 2 * step

    def up(c):
        for half in range(2):
            cols = slice(half * dff + c * fc, half * dff + (c + 1) * fc)
            slot = 2 * (c % 2) + half
            u_scr[slot, 0:halo_rows, :] = halo[:, cols]
            u_scr[slot, halo_rows:halo_rows + tm, :] = _dot(h, wup_ref[:, cols])

    up(0)
    for c in range(n_chunks):
        if c + 1 < n_chunks:
            up(c + 1)
        conv = []
        for half in range(2):
            cols = slice(half * dff + c * fc, half * dff + (c + 1) * fc)
            slot = 2 * (c % 2) + half
            conv.append(cw_ref[2:3, cols] * u_scr[slot, halo_rows:halo_rows + tm, :]
                        + cw_ref[1:2, cols] * u_scr[slot, o1:o1 + tm, :]
                        + cw_ref[0:1, cols] * u_scr[slot, o2:o2 + tm, :])
            halo[:, cols] = u_scr[slot, tm:tm + halo_rows, :]
        y = (conv[0] * _sigmoid(conv[0]) * conv[1]).astype(BF16)
        acc[...] += _dot(y, wdn_ref[c * fc:(c + 1) * fc, :])
    out = acc[...]
    if final:
        out = _rms(out, gf_ref[...])
    o_ref[0] = out
    st_ref[0] = halo[...]


def _ffn(x, g, w_up, conv_w, w_down, hist, tm, step, g_final=None):
    b, t, d = x.shape
    dff = w_down[0].shape[-2]
    halo_rows = hist.shape[1]
    tm = min(tm, t)

    def const(arr, lead):
        nl = len(lead)
        shape = tuple(arr.shape[nl:])
        return pl.BlockSpec((None,) * nl + shape, lambda *_: tuple(lead) + (0,) * len(shape),
                            pipeline_mode=pl.Buffered(1))

    in_specs = [pl.BlockSpec((1, tm, d), lambda bi, i: (bi, i, 0)), const(*g), const(*w_up), const(*conv_w),
                const(*w_down), pl.BlockSpec((1, halo_rows, 2 * dff), lambda bi, i: (bi, 0, 0))]
    args = [x, g[0], w_up[0], conv_w[0], w_down[0], hist]
    if g_final is not None:
        in_specs.append(const(g_final, ()))
        args.append(g_final)
    return pl.pallas_call(
        functools.partial(_ffn_kernel, tm=tm, halo_rows=halo_rows, step=step, final=g_final is not None),
        grid=(b, t // tm),
        in_specs=in_specs,
        out_specs=[pl.BlockSpec((1, tm, d), lambda bi, i: (bi, i, 0)),
                   pl.BlockSpec((1, halo_rows, 2 * dff), lambda bi, i: (bi, 0, 0))],
        out_shape=[jax.ShapeDtypeStruct((b, t, d), F32), jax.ShapeDtypeStruct((b, halo_rows, 2 * dff), F32)],
        scratch_shapes=[pltpu.VMEM((4, halo_rows + tm, FFN_CHUNK), F32), pltpu.VMEM((halo_rows, 2 * dff), F32),
                        pltpu.VMEM((tm, d), F32)],
        compiler_params=_cparams("parallel", "arbitrary"),
        name="conv_ffn",
    )(*args)


def _pad_rows_top(a, rows):
    return jnp.pad(a, ((0, 0), (rows - a.shape[1], 0), (0, 0)))


def _row3(a):
    return a.reshape(a.shape[0], 1, a.shape[1])


def kernel(x_prompt, x_sample, state_pool, state_dn_conv, state_dn, cache_win_w128, cache_win_w512, cache_win_w2048, cache_mem_k, cache_mem_v, state_ffn_conv, mem_prompt, g_mix, w_in_ab, w_pool, pool_scale, dn_conv_w, dn_a_log, dn_dt_bias, dn_norm_w, w_out_ab, w_qkv_c, w_out_c, g_mem_q, g_mem_kv, w_mem_q, w_mem_k, w_mem_v, w_mem_o, g_ffn, w_up, ffn_conv_w, w_down, g_final):
    bp, s, d = x_prompt.shape
    bs, ts, _ = x_sample.shape
    depth = g_mix.shape[0]
    n_mem = mem_prompt.shape[1]
    dff = w_down.shape[1]
    d_pool = len(POOL_WINDOWS) * LANES
    dd = DN_HEADS * DN_DIM
    hd = d // MEM_HEADS
    win_caches = (cache_win_w128, cache_win_w512, cache_win_w2048)

    n_in = w_in_ab.shape[2]
    n_pad = -(-n_in // LANES) * LANES
    w_in_b = jnp.pad(w_in_ab, ((0, 0), (0, 0), (0, n_pad - n_in))).astype(BF16)
    w_pool_b, w_out_ab_b = w_pool.astype(BF16), w_out_ab.astype(BF16)
    w_qkv_b, w_out_c_b = w_qkv_c.astype(BF16), w_out_c.astype(BF16)
    wq_b, wk_b, wv_b, wo_b = (w.astype(BF16) for w in (w_mem_q, w_mem_k, w_mem_v, w_mem_o))
    w_up_b, w_down_b = w_up.astype(BF16), w_down.astype(BF16)
    g_mix3, g_mem_q3, g_mem_kv3, g_ffn3 = _row3(g_mix), _row3(g_mem_q), _row3(g_mem_kv), _row3(g_ffn)
    pool_scale3, norm_w3 = _row3(pool_scale), _row3(dn_norm_w)
    lane_pad = ((0, 0), (DN_HEADS, LANES - 2 * DN_HEADS))
    alog3, dtb3 = _row3(jnp.pad(dn_a_log, lane_pad)), _row3(jnp.pad(dn_dt_bias, lane_pad))
    g_final2 = g_final.reshape(1, d)

    xp = x_prompt.reshape(bp * s, d)
    xs = x_sample.reshape(bs * ts, d)
    mem_flat = mem_prompt.reshape(bp * n_mem, d)

    pool_p, pool_s, dconv_p, dconv_s, dn_p, dn_s = [], [], [], [], [], []
    win_p = [[] for _ in WIN_CONFIGS]
    win_s = [[] for _ in WIN_CONFIGS]
    memk_p, memv_p, fconv_p, fconv_s = [], [], [], []

    for layer in range(depth):
        if layer % 2 == 0:
            e = layer // 2
            zero_state = jnp.zeros((1, bp, DN_HEADS, DN_DIM, DN_DIM), F32)
            new_x = []
            for x, b, t, hist_pool, hist_conv, s0, p0 in (
                    (xp, bp, s, jnp.zeros((bp, POOL_HALO, d_pool), F32), jnp.zeros((bp, DN_HALO, 3 * dd), F32),
                     (zero_state, (0,)), 0),
                    (xs, bs, ts, _pad_rows_top(state_pool[e], POOL_HALO), _pad_rows_top(state_dn_conv[e], DN_HALO),
                     (state_dn, (e,)), PAST_LEN)):
                proj, = _norm_proj(x, (g_mix3, (layer,)), [(w_in_b, (e,))], [F32], tm=256)
                proj = proj.reshape(b, t, n_pad)
                y_pool = _pool(proj, hist_pool, (w_pool_b, (e,)), (pool_scale3, (e,)), tm=512, p0=p0)
                o_dn, s_new = _delta(proj, hist_conv, s0, (dn_conv_w, (e,)), (alog3, (e,)), (dtb3, (e,)),
                                     (norm_w3, (e,)))
                x = _proj_res([y_pool.reshape(b * t, d_pool), o_dn.reshape(b * t, dd)],
                              [(w_out_ab_b, (e,), 0), (w_out_ab_b, (e,), 1)], x, tm=512)
                new_x.append((x, proj, s_new))
            (xp, proj_p, s_p), (xs, proj_s, s_s) = new_x
            pool_p.append(proj_p[:, s - (POOL_HALO - 1):, :d_pool])
            pool_s.append(jnp.concatenate([state_pool[e], proj_s[:, :, :d_pool]], axis=1)[:, -(POOL_HALO - 1):])
            dconv_p.append(proj_p[:, s - (DN_CONV - 1):, d_pool:d_pool + 3 * dd])
            dconv_s.append(jnp.concatenate([state_dn_conv[e], proj_s[:, :, d_pool:d_pool + 3 * dd]],
                                           axis=1)[:, -(DN_CONV - 1):])
            dn_p.append(s_p)
            dn_s.append(s_s)
        else:
            o = layer // 2
            c = w_qkv_c.shape[2]
            qkv_p = _qkv_rope(xp, (g_mix3, (layer,)), (w_qkv_b, (o,)), _rope_tables(jnp.arange(s)), tm=256)
            tab_s = tuple(jnp.tile(u, (bs, 1)) for u in _rope_tables(PAST_LEN + jnp.arange(ts)))
            qkv_s = _qkv_rope(xs, (g_mix3, (layer,)), (w_qkv_b, (o,)), tab_s, tm=256)
            qkv_p3 = qkv_p.reshape(bp, s, c)
            qkv_s3 = qkv_s.reshape(bs, ts, c)
            xp = _proj_res([_swa_prompt(qkv_p3)], [(w_out_c_b, (o,), 0)], xp, tm=512)
            caches = [ch.transpose(0, 1, 3, 4, 5, 2).reshape(ch.shape[:2] + (2, D_SWA, ch.shape[2]))
                      for ch in win_caches]
            o_s = _swa_sample(qkv_s3, caches, o)
            xs = _proj_res([o_s.reshape(bs * ts, D_SWA)], [(w_out_c_b, (o,), 0)], xs, tm=512)
            for gi, (win, _) in enumerate(WIN_CONFIGS):
                base = gi * 3 * D_SWA
                keep = min(win, s)
                kv_p = qkv_p3[:, s - keep:, base + D_SWA:base + 3 * D_SWA]
                win_p[gi].append(kv_p.reshape(bp, keep, 2, SWA_HEADS, SWA_DIM))
                kv_s = qkv_s3[:, :, base + D_SWA:base + 3 * D_SWA]
                win_s[gi].append(kv_s.reshape(bs, ts, 2, SWA_HEADS, SWA_DIM))

        mk, mv = _norm_proj(mem_flat, (g_mem_kv3, (layer,)), [(wk_b, (layer,)), (wv_b, (layer,))], [F32, F32], tm=256)
        memk_p.append(mk.reshape(bp, n_mem, MEM_HEADS, hd))
        memv_p.append(mv.reshape(bp, n_mem, MEM_HEADS, hd))
        xp = _mem_block(xp.reshape(bp, s, d), (g_mem_q3, (layer,)), (wq_b, (layer,)), mk.reshape(bp, n_mem, d),
                        mv.reshape(bp, n_mem, d), (wo_b, (layer,)), tm=512).reshape(bp * s, d)
        q_s, = _norm_proj(xs, (g_mem_q3, (layer,)), [(wq_b, (layer,))], [BF16], tm=512)
        att_s = _mem_attn(q_s.reshape(bs, ts, d), cache_mem_k, cache_mem_v, layer)
        xs = _proj_res([att_s.reshape(bs * ts, d)], [(wo_b, (layer,), 0)], xs, tm=512)

        gf = g_final2 if layer == depth - 1 else None
        ffn_w = ((g_ffn3, (layer,)), (w_up_b, (layer,)), (ffn_conv_w, (layer,)), (w_down_b, (layer,)))
        yp, st_p = _ffn(xp.reshape(bp, s, d), *ffn_w, jnp.zeros((bp, 8, 2 * dff), F32), tm=512, step=1, g_final=gf)
        xp = yp.reshape(bp * s, d)
        fconv_p.append(st_p[:, -2:])
        xs_tm = xs.reshape(bs, ts, d).transpose(1, 0, 2).reshape(1, ts * bs, d)
        hist_tm = state_ffn_conv[layer].transpose(1, 0, 2).reshape(1, 2 * bs, 2 * dff)
        ys, st_s = _ffn(xs_tm, *ffn_w, hist_tm, tm=ts * bs, step=bs, g_final=gf)
        xs = ys.reshape(ts, bs, d).transpose(1, 0, 2).reshape(bs * ts, d)
        fconv_s.append(st_s.reshape(2, bs, 2 * dff).transpose(1, 0, 2))

    return (xp.reshape(bp, s, d), xs.reshape(bs, ts, d),
            jnp.stack(pool_p), jnp.stack(pool_s),
            jnp.stack(dconv_p), jnp.stack(dconv_s),
            jnp.stack(dn_p), jnp.stack(dn_s),
            jnp.stack(win_p[0]), jnp.stack(win_s[0]),
            jnp.stack(win_p[1]), jnp.stack(win_s[1]),
            jnp.stack(win_p[2]), jnp.stack(win_s[2]),
            jnp.stack(memk_p), jnp.stack(memv_p),
            jnp.stack(fconv_p), jnp.stack(fconv_s))
```

```python
import functools

import jax
import jax.numpy as jnp
from jax import lax
from jax.experimental import pallas as pl
from jax.experimental.pallas import tpu as pltpu

F32 = jnp.float32
BF16 = jnp.bfloat16
EPS = 1e-6

PAST_LEN = 8192
POOL_WINDOWS = (2, 4, 8, 16)
POOL_HALO = 16
DN_HEADS = 4
DN_DIM = 128
DN_CONV = 4
DN_CHUNK = 128
DN_HALO = 8
DN_BATCH = 4
WIN_CONFIGS = ((128, 1), (512, 4), (2048, 16))
SWA_HEADS = 8
SWA_DIM = 64
D_SWA = SWA_HEADS * SWA_DIM
SWA_BLOCK = 128
SWA_STAGE_BLOCKS = 16
ROT_HALF = 8
ROPE_THETA = 500000.0
LOG2_E = 1.4426950408889634
MEM_HEADS = 4
FFN_CHUNK = 256
LANES = 128
VMEM_LIMIT_BYTES = 56 * 1024 * 1024


def _cparams(*sem):
    return pltpu.CompilerParams(dimension_semantics=sem, vmem_limit_bytes=VMEM_LIMIT_BYTES)


def _rms(x, g):
    return x * lax.rsqrt(jnp.mean(x * x, axis=-1, keepdims=True) + EPS) * g


def _sigmoid(x):
    return 1.0 / (1.0 + jnp.exp(-x))


def _dot(a, b):
    return jnp.dot(a, b, preferred_element_type=F32)


def _dot_nt(a, b):
    return lax.dot_general(a, b, (((1,), (1,)), ((), ())), preferred_element_type=F32)


def _dot_tn(a, b):
    return lax.dot_general(a, b, (((0,), (0,)), ((), ())), preferred_element_type=F32)


def _sel(arr, lead, block=None, index=None):
    nl = len(lead)
    shape = tuple(arr.shape[nl:]) if block is None else tuple(block)
    idx = (0,) * len(shape) if index is None else tuple(index)
    return pl.BlockSpec((None,) * nl + shape, lambda *_: tuple(lead) + idx)


def _norm_proj_kernel(x_ref, g_ref, *refs, n_w, tn):
    w_refs, o_refs = refs[:n_w], refs[n_w:]
    h = _rms(x_ref[...], g_ref[...]).astype(BF16)
    for w_ref, o_ref in zip(w_refs, o_refs):
        n = w_ref.shape[1]
        for j in range(0, n, tn):
            jw = min(tn, n - j)
            o_ref[:, j:j + jw] = _dot(h, w_ref[:, j:j + jw]).astype(o_ref.dtype)


def _norm_proj(x, g, ws, out_dtypes, tm):
    m, d = x.shape
    tm = min(tm, m)
    g_arr, g_lead = g
    in_specs = [pl.BlockSpec((tm, d), lambda i: (i, 0)), _sel(g_arr, g_lead)]
    in_specs += [_sel(w, lead) for w, lead in ws]
    widths = [w.shape[-1] for w, _ in ws]
    return pl.pallas_call(
        functools.partial(_norm_proj_kernel, n_w=len(ws), tn=512),
        grid=(m // tm,),
        in_specs=in_specs,
        out_specs=[pl.BlockSpec((tm, n), lambda i: (i, 0)) for n in widths],
        out_shape=[jax.ShapeDtypeStruct((m, n), dt) for n, dt in zip(widths, out_dtypes)],
        compiler_params=_cparams("parallel"),
        name="norm_proj",
    )(x, g_arr, *[w for w, _ in ws])


def _qkv_rope_kernel(x_ref, g_ref, w_ref, cos_ref, sa_ref, sb_ref, o_ref):
    h = _rms(x_ref[...], g_ref[...]).astype(BF16)
    cos, sa, sb = cos_ref[...], sa_ref[...], sb_ref[...]
    for piece in range(w_ref.shape[1] // D_SWA):
        c0 = piece * D_SWA
        y = _dot(h, w_ref[:, c0:c0 + D_SWA])
        if piece % 3 == 2:
            o_ref[:, c0:c0 + D_SWA] = y
            continue
        for a in range(D_SWA // LANES):
            ya = y[:, a * LANES:(a + 1) * LANES]
            rot = (ya * cos + pltpu.roll(ya, LANES - ROT_HALF, axis=1) * sa
                   + pltpu.roll(ya, ROT_HALF, axis=1) * sb)
            o_ref[:, c0 + a * LANES:c0 + (a + 1) * LANES] = rot


def _rope_tables(pos):
    t = pos.shape[0]
    inv_freq = ROPE_THETA ** (-jnp.arange(ROT_HALF, dtype=F32) / ROT_HALF)
    ang = pos.astype(F32)[:, None] * inv_freq[None, :]
    cos, sin = jnp.cos(ang), jnp.sin(ang)
    rest = SWA_DIM - 2 * ROT_HALF
    c64 = jnp.concatenate([cos, cos, jnp.ones((t, rest), F32)], axis=1)
    a64 = jnp.concatenate([-sin, jnp.zeros((t, SWA_DIM - ROT_HALF), F32)], axis=1)
    b64 = jnp.concatenate([jnp.zeros((t, ROT_HALF), F32), sin, jnp.zeros((t, rest), F32)], axis=1)
    return tuple(jnp.concatenate([u, u], axis=1) for u in (c64, a64, b64))


def _qkv_rope(x, g, w, tables, tm):
    m, d = x.shape
    w_arr, w_lead = w
    g_arr, g_lead = g
    n = w_arr.shape[-1]
    tm = min(tm, m)
    nper = tables[0].shape[0] // tm
    tab_spec = pl.BlockSpec((tm, LANES), lambda i: (i % nper, 0))
    return pl.pallas_call(
        _qkv_rope_kernel,
        grid=(m // tm,),
        in_specs=[pl.BlockSpec((tm, d), lambda i: (i, 0)), _sel(g_arr, g_lead), _sel(w_arr, w_lead),
                  tab_spec, tab_spec, tab_spec],
        out_specs=pl.BlockSpec((tm, n), lambda i: (i, 0)),
        out_shape=jax.ShapeDtypeStruct((m, n), F32),
        compiler_params=_cparams("parallel"),
        name="qkv_rope",
    )(x, g_arr, w_arr, *tables)


def _proj_res_kernel(*refs, n_in):
    a_refs, w_refs = refs[:n_in], refs[n_in:2 * n_in]
    x_ref, o_ref = refs[2 * n_in], refs[2 * n_in + 1]
    acc = x_ref[...]
    for a_ref, w_ref in zip(a_refs, w_refs):
        acc = acc + _dot(a_ref[...], w_ref[...])
    o_ref[...] = acc


def _proj_res(acts, ws, x, tm):
    m, d = x.shape
    tm = min(tm, m)
    in_specs = [pl.BlockSpec((tm, a.shape[1]), lambda i: (i, 0)) for a in acts]
    in_specs += [_sel(w, lead, block=(a.shape[1], d), index=(blk, 0)) for a, (w, lead, blk) in zip(acts, ws)]
    in_specs += [pl.BlockSpec((tm, d), lambda i: (i, 0))]
    return pl.pallas_call(
        functools.partial(_proj_res_kernel, n_in=len(acts)),
        grid=(m // tm,),
        in_specs=in_specs,
        out_specs=pl.BlockSpec((tm, d), lambda i: (i, 0)),
        out_shape=jax.ShapeDtypeStruct((m, d), F32),
        compiler_params=_cparams("parallel"),
        name="proj_res",
    )(*acts, *[w for w, _, _ in ws], x)


def _pool_kernel(a_ref, hist_ref, w_ref, sc_ref, *rest, tm, p0, fused):
    if fused:
        odn_ref, wa_ref, wb_ref, x_ref, o_ref, buf = rest
    else:
        o_ref, buf = rest
    i = pl.program_id(1)

    @pl.when(i == 0)
    def _():
        buf[0:POOL_HALO, :] = hist_ref[0]

    @pl.when(i > 0)
    def _():
        buf[0:POOL_HALO, :] = buf[tm:tm + POOL_HALO, :]

    buf[POOL_HALO:POOL_HALO + tm, :] = a_ref[0]
    pos = p0 + i * tm + lax.broadcasted_iota(jnp.int32, (tm, 1), 0)
    ys = []
    for gi, win in enumerate(POOL_WINDOWS):
        cs = slice(gi * LANES, (gi + 1) * LANES)
        cur = buf[POOL_HALO:POOL_HALO + tm, cs]
        tot = cur
        for j in range(1, win):
            tot = tot + buf[POOL_HALO - j:POOL_HALO - j + tm, cs]
        cnt = jnp.minimum(pos + 1, win).astype(F32)
        z = tot / cnt - cur
        ys.append((_dot(z.astype(BF16), w_ref[gi]) * sc_ref[:, cs]).astype(BF16))
    y = jnp.concatenate(ys, axis=1)
    if fused:
        o_ref[0] = x_ref[0] + _dot(y, wa_ref[...]) + _dot(odn_ref[0], wb_ref[...])
    else:
        o_ref[0] = y


def _pool(proj, hist, w, scale, tm, p0, fuse=None):
    b, t, _ = proj.shape
    dp = len(POOL_WINDOWS) * LANES
    tm = min(tm, t)
    in_specs = [pl.BlockSpec((1, tm, dp), lambda bi, i: (bi, i, 0)),
                pl.BlockSpec((1, POOL_HALO, dp), lambda bi, i: (bi, 0, 0)),
                _sel(*w), _sel(*scale)]
    args = [proj, hist, w[0], scale[0]]
    out_w, out_dt = dp, BF16
    if fuse is not None:
        o_dn, (w_out, lead), x = fuse
        dd, d = o_dn.shape[2], x.shape[2]
        in_specs += [pl.BlockSpec((1, tm, dd), lambda bi, i: (bi, i, 0)),
                     _sel(w_out, lead, block=(dp, d), index=(0, 0)),
                     _sel(w_out, lead, block=(dd, d), index=(dp // dd, 0)),
                     pl.BlockSpec((1, tm, d), lambda bi, i: (bi, i, 0))]
        args += [o_dn, w_out, w_out, x]
        out_w, out_dt = d, F32
    return pl.pallas_call(
        functools.partial(_pool_kernel, tm=tm, p0=p0, fused=fuse is not None),
        grid=(b, t // tm),
        in_specs=in_specs,
        out_specs=pl.BlockSpec((1, tm, out_w), lambda bi, i: (bi, i, 0)),
        out_shape=jax.ShapeDtypeStruct((b, t, out_w), out_dt),
        scratch_shapes=[pltpu.VMEM((POOL_HALO + tm, dp), F32)],
        compiler_params=_cparams("parallel", "arbitrary"),
        name="pool_mix",
    )(*args)


def _cumsum_rows(x):
    n = x.shape[0]
    row = lax.broadcasted_iota(jnp.int32, x.shape, 0)
    sh = 1
    while sh < n:
        x = x + jnp.where(row >= sh, pltpu.roll(x, sh, axis=0), 0.0)
        sh *= 2
    return x


def _delta_kernel(q_ref, k_ref, v_ref, gate_ref, ba_ref, hist_ref, s0_ref, cw_ref, alog_ref, dtb_ref,
                  nw_ref, o_ref, s_ref, ext, *, rows, nb):
    c = pl.program_id(1)
    C = DN_CHUNK
    dd = DN_HEADS * DN_DIM

    @pl.when(c == 0)
    def _():
        ext[:, 0:DN_HALO, :] = hist_ref[...]
        s_ref[...] = s0_ref[...]

    ri = lax.broadcasted_iota(jnp.int32, (C, C), 0)
    ci = lax.broadcasted_iota(jnp.int32, (C, C), 1)
    incl = ri >= ci
    strict = ri > ci
    eye = (ri == ci).astype(F32)
    valid = lax.broadcasted_iota(jnp.int32, (C, LANES), 0) < rows

    chains = []
    for bi in range(nb):
        ext[bi, DN_HALO:DN_HALO + rows, 0:dd] = q_ref[bi]
        ext[bi, DN_HALO:DN_HALO + rows, dd:2 * dd] = k_ref[bi]
        ext[bi, DN_HALO:DN_HALO + rows, 2 * dd:3 * dd] = v_ref[bi]
        if rows < C:
            ext[bi, DN_HALO + rows:DN_HALO + C, :] = jnp.zeros((C - rows, 3 * dd), F32)
        conv = cw_ref[DN_CONV - 1:DN_CONV, :] * ext[bi, DN_HALO:DN_HALO + C, :]
        for kk in range(DN_CONV - 1):
            off = DN_HALO - (DN_CONV - 1) + kk
            conv = conv + cw_ref[kk:kk + 1, :] * ext[bi, off:off + C, :]
        act = conv * _sigmoid(conv)
        ext[bi, 0:DN_HALO, :] = ext[bi, rows:rows + DN_HALO, :]

        ba = ba_ref[bi]
        gate = gate_ref[bi]
        if rows < C:
            ba = jnp.concatenate([ba, jnp.zeros((C - rows, LANES), F32)], axis=0)
        beta_t = jnp.where(valid, _sigmoid(ba), 0.0)
        xg = ba + dtb_ref[...]
        softplus = jnp.maximum(xg, 0.0) + jnp.log(1.0 + jnp.exp(-jnp.abs(xg)))
        g_t = jnp.where(valid, -jnp.exp(alog_ref[...]) * softplus, 0.0)
        gcum = _cumsum_rows(g_t)
        gcum_t = gcum.T
        e_g = jnp.exp(gcum)
        g_last = gcum[C - 1:C, :]
        e_rev = jnp.exp(g_last - gcum)
        e_last = jnp.exp(g_last)

        for h in range(DN_HEADS):
            hs = slice(h * DN_DIM, (h + 1) * DN_DIM)
            qh = act[:, hs]
            kh = act[:, dd + h * DN_DIM:dd + (h + 1) * DN_DIM]
            vh = act[:, 2 * dd + h * DN_DIM:2 * dd + (h + 1) * DN_DIM]
            qn = qh * lax.rsqrt(jnp.sum(qh * qh, axis=-1, keepdims=True) + EPS) * (DN_DIM ** -0.5)
            kn = kh * lax.rsqrt(jnp.sum(kh * kh, axis=-1, keepdims=True) + EPS)
            beta = beta_t[:, h:h + 1]
            gcol = gcum[:, DN_HEADS + h:DN_HEADS + h + 1]
            grow = gcum_t[DN_HEADS + h:DN_HEADS + h + 1, :]
            eg = e_g[:, DN_HEADS + h:DN_HEADS + h + 1]
            chains.append(dict(
                bi=bi, h=h, hs=hs, qn=qn, kn=kn, kb=kn.astype(BF16), beta=beta,
                decay=jnp.where(incl, jnp.exp(gcol - grow), 0.0),
                rhs_u=(vh * beta).astype(BF16), rhs_w=(kn * (beta * eg)).astype(BF16),
                qg=(qn * eg).astype(BF16), kg=(kn * e_rev[:, DN_HEADS + h:DN_HEADS + h + 1]).astype(BF16),
                el=e_last[:, DN_HEADS + h:DN_HEADS + h + 1], gate=gate[:, hs]))

    for ch in chains:
        ch["a"] = jnp.where(strict, _dot_nt(ch["kb"], ch["kb"]) * ch["decay"] * ch["beta"], 0.0)
        a_blk = jnp.where((ri >> 4) == (ci >> 4), ch["a"], 0.0)
        ch["p"] = eye - a_blk
        ch["xp"] = a_blk
    for _ in range(3):
        for ch in chains:
            xb = ch["xp"].astype(BF16)
            ch["xp"] = _dot(xb, xb)
        for ch in chains:
            ch["p"] = ch["p"] + _dot(ch["p"].astype(BF16), ch["xp"].astype(BF16))
    sh = 4
    while (1 << sh) < C:
        off_mask = ((ri >> (sh + 1)) == (ci >> (sh + 1))) & ((ri >> sh) != (ci >> sh))
        for ch in chains:
            ch["pb"] = ch["p"].astype(BF16)
            ch["t"] = _dot(ch["pb"], jnp.where(off_mask, ch["a"], 0.0).astype(BF16))
        for ch in chains:
            ch["p"] = ch["p"] - _dot(ch["t"].astype(BF16), ch["pb"])
        sh += 1
    for ch in chains:
        pb = ch["p"].astype(BF16)
        ch["u"] = _dot(pb, ch["rhs_u"])
        ch["w"] = _dot(pb, ch["rhs_w"])
        ch["qk"] = (_dot_nt(ch["qn"].astype(BF16), ch["kb"]) * ch["decay"]).astype(BF16)
    for ch in chains:
        ch["s"] = s_ref[ch["bi"], ch["h"]]
        ch["sb"] = ch["s"].astype(BF16)
        ch["vb"] = (ch["u"] - _dot(ch["w"].astype(BF16), ch["sb"])).astype(BF16)
    for ch in chains:
        ch["o"] = _dot(ch["qg"], ch["sb"]) + _dot(ch["qk"], ch["vb"])
        s_ref[ch["bi"], ch["h"]] = ch["s"] * ch["el"] + _dot_tn(ch["kg"], ch["vb"])
    for ch in chains:
        o = ch["o"]
        o = o * lax.rsqrt(jnp.mean(o * o, axis=-1, keepdims=True) + EPS) * nw_ref[...]
        o = o[0:rows] * (ch["gate"] * _sigmoid(ch["gate"]))
        o_ref[ch["bi"], :, ch["hs"]] = o.astype(o_ref.dtype)


def _delta(proj, hist, s0, conv_w, alog_row, dtb_row, norm_w):
    b, t, _ = proj.shape
    dd = DN_HEADS * DN_DIM
    rows = min(DN_CHUNK, t)
    nb = DN_BATCH
    s_arr, s_lead = s0
    col = lambda j: pl.BlockSpec((nb, rows, dd), lambda bi, c: (bi, c, j))
    st_block = (nb, DN_HEADS, DN_DIM, DN_DIM)
    return pl.pallas_call(
        functools.partial(_delta_kernel, rows=rows, nb=nb),
        grid=(b // nb, t // rows),
        in_specs=[col(1), col(2), col(3), col(4),
                  pl.BlockSpec((nb, rows, LANES), lambda bi, c: (bi, c, 5 * dd // LANES)),
                  pl.BlockSpec((nb, DN_HALO, 3 * dd), lambda bi, c: (bi, 0, 0)),
                  pl.BlockSpec((None,) * len(s_lead) + st_block, lambda bi, c: tuple(s_lead) + (bi, 0, 0, 0)),
                  _sel(*conv_w), _sel(*alog_row), _sel(*dtb_row), _sel(*norm_w)],
        out_specs=[pl.BlockSpec((nb, rows, dd), lambda bi, c: (bi, c, 0)),
                   pl.BlockSpec(st_block, lambda bi, c: (bi, 0, 0, 0))],
        out_shape=[jax.ShapeDtypeStruct((b, t, dd), BF16),
                   jax.ShapeDtypeStruct((b, DN_HEADS, DN_DIM, DN_DIM), F32)],
        scratch_shapes=[pltpu.VMEM((nb, DN_HALO + DN_CHUNK, 3 * dd), F32)],
        compiler_params=_cparams("parallel", "arbitrary"),
        name="delta_rule",
    )(proj, proj, proj, proj, proj, hist, s_arr, conv_w[0], alog_row[0], dtb_row[0], norm_w[0])


def _swa_group_bodies(n, q_ref, k_ref, v_ref, num_s, mx_s, den_s, gi, kst, vst, kpv, vpv, dil, nblk):
    blk = SWA_BLOCK
    span = blk * nblk

    def rows_of(r, first_blk, n_blk):
        if dil == 1:
            return pl.ds(first_blk * blk, n_blk * blk)
        return pl.ds(r + dil * blk * first_blk, n_blk * blk, stride=dil)

    @pl.when(n == 0)
    def _():
        kpv[...] = jnp.zeros_like(kpv)
        vpv[...] = jnp.zeros_like(vpv)

    for r in range(dil):
        kst[r] = k_ref[0, rows_of(r, 0, nblk), :].astype(BF16)
        vst[r] = v_ref[0, rows_of(r, 0, nblk), :].astype(BF16)

    qi = lax.broadcasted_iota(jnp.int32, (blk, blk), 0)
    kj = lax.broadcasted_iota(jnp.int32, (blk, blk), 1)
    below = kj <= qi
    above = kj >= qi
    above_first = above & (n > 0)
    lane = lax.broadcasted_iota(jnp.int32, (1, LANES), 1)
    zero = jnp.zeros((), BF16)
    scale = SWA_DIM ** -0.5 * LOG2_E

    blocks = [(r, j) for r in range(dil) for j in range(nblk)]
    for g0 in range(0, len(blocks), SWA_STAGE_BLOCKS):
        bodies = []
        for r, j in blocks[g0:g0 + SWA_STAGE_BLOCKS]:
            rs = slice(j * blk, (j + 1) * blk)
            q_b = (q_ref[0, rows_of(r, j, 1), :] * scale).astype(BF16)
            if j == 0:
                kp, vp, prev_ok = kpv[r], vpv[r], above_first
            else:
                ps = slice((j - 1) * blk, j * blk)
                kp, vp, prev_ok = kst[r, ps, :], vst[r, ps, :], above
            kc, vc = kst[r, rs, :], vst[r, rs, :]
            for half in range(2):
                hm = (lane < SWA_DIM) if half == 0 else (lane >= SWA_DIM)
                qh = jnp.where(hm, q_b, zero)
                bodies.append(dict(
                    r=r, j=j, half=half, hm=hm, vp=vp, vc=vc,
                    s_p=jnp.where(prev_ok, _dot_nt(qh, kp), -jnp.inf),
                    s_c=jnp.where(below, _dot_nt(qh, kc), -jnp.inf)))
        for bd in bodies:
            bd["m"] = jnp.max(jnp.maximum(bd["s_p"], bd["s_c"]), axis=-1, keepdims=True)
        for bd in bodies:
            e_p = jnp.exp2(bd["s_p"] - bd["m"])
            e_c = jnp.exp2(bd["s_c"] - bd["m"])
            bd["den"] = jnp.sum(e_p + e_c, axis=-1, keepdims=True)
            bd["e_p"], bd["e_c"] = e_p.astype(BF16), e_c.astype(BF16)
        for b0, b1 in zip(bodies[0::2], bodies[1::2]):
            acc = None
            for bd in (b0, b1):
                part = (_dot(bd["e_p"], jnp.where(bd["hm"], bd["vp"], zero))
                        + _dot(bd["e_c"], jnp.where(bd["hm"], bd["vc"], zero)))
                acc = part if acc is None else acc + part
            rows = rows_of(b0["r"], b0["j"], 1)
            num_s[gi, rows, :] = acc
            mx_s[gi, rows, :] = jnp.where(b0["hm"], b0["m"], b1["m"])
            den_s[gi, rows, :] = jnp.where(b0["hm"], b0["den"], b1["den"])

    for r in range(dil):
        kpv[r] = kst[r, span - blk:span, :]
        vpv[r] = vst[r, span - blk:span, :]


def _swa_prompt_kernel(*refs, dils, rows):
    ng = len(dils)
    in_refs, o_ref, scr = refs[:3 * ng], refs[3 * ng], refs[3 * ng + 1:]
    num_s, mx_s, den_s = scr[4 * ng:]
    n = pl.program_id(2)
    for gi, dil in enumerate(dils):
        _swa_group_bodies(n, *in_refs[3 * gi:3 * gi + 3], num_s, mx_s, den_s, gi, *scr[4 * gi:4 * gi + 4],
                          dil, rows // (SWA_BLOCK * dil))
    chunk = 2 * SWA_BLOCK
    for c0 in range(0, rows, chunk):
        rs = slice(c0, c0 + chunk)
        ms = [mx_s[gi, rs, :] for gi in range(ng)]
        mx = functools.reduce(jnp.maximum, ms)
        ws = [jnp.exp2(m - mx) for m in ms]
        tot = sum(w * den_s[gi, rs, :] for gi, w in enumerate(ws))
        o = sum((w / tot) * num_s[gi, rs, :] for gi, w in enumerate(ws))
        o_ref[0, rs, :] = o.astype(o_ref.dtype)


def _swa_prompt(qkv):
    b, s, _ = qkv.shape
    dils = tuple(dil for _, dil in WIN_CONFIGS)
    rows = SWA_BLOCK * max(dils)
    npair = D_SWA // LANES
    in_specs, scratch = [], []
    for gi, dil in enumerate(dils):
        for j in range(3):
            in_specs.append(pl.BlockSpec((1, rows, LANES),
                                         lambda bi, pp, n, gi=gi, j=j: (bi, n, (3 * gi + j) * npair + pp)))
        stage = pltpu.VMEM((dil, rows // dil, LANES), BF16)
        prev = pltpu.VMEM((dil, SWA_BLOCK, LANES), BF16)
        scratch += [stage, stage, prev, prev]
    scratch += [pltpu.VMEM((len(dils), rows, LANES), F32)] * 3
    out = pl.pallas_call(
        functools.partial(_swa_prompt_kernel, dils=dils, rows=rows),
        grid=(b, npair, s // rows),
        in_specs=in_specs,
        out_specs=pl.BlockSpec((1, rows, LANES), lambda bi, pp, n: (bi, n, pp)),
        out_shape=jax.ShapeDtypeStruct((b, s, D_SWA), BF16),
        scratch_shapes=scratch,
        compiler_params=_cparams("parallel", "parallel", "arbitrary"),
        name="swa_prompt",
    )(*([qkv] * (3 * len(dils))))
    return out.reshape(b * s, D_SWA)


def _swa_sample_kernel(qkv_ref, c0_ref, c1_ref, c2_ref, o_ref, *, t_new):
    caches = (c0_ref, c1_ref, c2_ref)
    rows = SWA_HEADS * t_new
    rh = jnp.concatenate([jnp.full((t_new, D_SWA), h, jnp.int32) for h in range(SWA_HEADS)], axis=0)
    chd = lax.broadcasted_iota(jnp.int32, (rows, D_SWA), 1) >> 6
    head_mask = rh == chd
    scale = SWA_DIM ** -0.5
    pad = jnp.zeros((LANES - t_new, D_SWA), BF16)
    pieces = []
    for gi, (win, dil) in enumerate(WIN_CONFIGS):
        c_ref = caches[gi]
        n_hist = c_ref.shape[2]
        base = gi * 3 * D_SWA
        q = qkv_ref[0, :, base:base + D_SWA] * scale
        qrows = jnp.where(head_mask, jnp.concatenate([q] * SWA_HEADS, axis=0), 0.0).astype(BF16)
        tok = jnp.concatenate([lax.broadcasted_iota(jnp.int32, (t_new, n_hist), 0)] * SWA_HEADS, axis=0)
        dist = n_hist + tok - lax.broadcasted_iota(jnp.int32, (rows, n_hist), 1)
        ok = (dist <= win) & ((dist & (dil - 1)) == 0)
        s_c = jnp.where(ok, _dot(qrows, c_ref[0].astype(BF16)), -jnp.inf)
        k_new = jnp.concatenate([qkv_ref[0, :, base + D_SWA:base + 2 * D_SWA].astype(BF16), pad], axis=0)
        v_new = jnp.concatenate([qkv_ref[0, :, base + 2 * D_SWA:base + 3 * D_SWA].astype(BF16), pad], axis=0)
        tok_n = jnp.concatenate([lax.broadcasted_iota(jnp.int32, (t_new, LANES), 0)] * SWA_HEADS, axis=0)
        dn = tok_n - lax.broadcasted_iota(jnp.int32, (rows, LANES), 1)
        ok_n = (dn >= 0) & (dn <= win) & ((dn & (dil - 1)) == 0)
        s_n = jnp.where(ok_n, _dot_nt(qrows, k_new), -jnp.inf)
        pieces.append((s_c, s_n, c_ref, v_new))
    m = None
    for s_c, s_n, _, _ in pieces:
        mm = jnp.maximum(jnp.max(s_c, axis=-1, keepdims=True), jnp.max(s_n, axis=-1, keepdims=True))
        m = mm if m is None else jnp.maximum(m, mm)
    den = jnp.zeros((rows, 1), F32)
    num = jnp.zeros((rows, D_SWA), F32)
    for s_c, s_n, c_ref, v_new in pieces:
        e_c = jnp.exp(s_c - m)
        e_n = jnp.exp(s_n - m)
        den = den + jnp.sum(e_c, axis=-1, keepdims=True) + jnp.sum(e_n, axis=-1, keepdims=True)
        num = num + _dot_nt(e_c.astype(BF16), c_ref[1].astype(BF16)) + _dot(e_n.astype(BF16), v_new)
    o = jnp.where(head_mask, num / den, 0.0)
    out = o[0:t_new]
    for h in range(1, SWA_HEADS):
        out = out + o[h * t_new:(h + 1) * t_new]
    o_ref[0] = out.astype(o_ref.dtype)


def _swa_sample(qkv, caches, layer_idx):
    b, t, c = qkv.shape
    return pl.pallas_call(
        functools.partial(_swa_sample_kernel, t_new=t),
        grid=(b,),
        in_specs=[pl.BlockSpec((1, t, c), lambda bi: (bi, 0, 0))]
        + [pl.BlockSpec((None, None) + ch.shape[2:], lambda bi: (layer_idx, bi, 0, 0, 0)) for ch in caches],
        out_specs=pl.BlockSpec((1, t, D_SWA), lambda bi: (bi, 0, 0)),
        out_shape=jax.ShapeDtypeStruct((b, t, D_SWA), BF16),
        compiler_params=_cparams("parallel"),
        name="swa_sample",
    )(qkv, *caches)


def _mem_attn_kernel(q_ref, k_ref, v_ref, o_ref):
    n, nh, hd = k_ref.shape
    t = q_ref.shape[1]
    q = q_ref[0]
    qs = jnp.concatenate([q[:, h * hd:(h + 1) * hd] for h in range(nh)], axis=0)
    k_all = k_ref[...].reshape(n * nh, hd).astype(BF16)
    v_all = v_ref[...].reshape(n * nh, hd).astype(BF16)
    row_head = jnp.concatenate([jnp.full((t, n * nh), h, jnp.int32) for h in range(nh)], axis=0)
    col_head = lax.broadcasted_iota(jnp.int32, (nh * t, n * nh), 1) & (nh - 1)
    s = jnp.where(row_head == col_head, _dot_nt(qs, k_all) * (hd ** -0.5), -jnp.inf)
    e = jnp.exp(s - jnp.max(s, axis=-1, keepdims=True))
    o = _dot(e.astype(BF16), v_all) / jnp.sum(e, axis=-1, keepdims=True)
    o_ref[0] = jnp.concatenate([o[h * t:(h + 1) * t] for h in range(nh)], axis=1).astype(o_ref.dtype)


def _mem_attn(q, k, v, layer_idx):
    b, t, d = q.shape
    kv_spec = pl.BlockSpec((None, None) + k.shape[2:], lambda bi: (layer_idx, bi, 0, 0, 0))
    return pl.pallas_call(
        _mem_attn_kernel,
        grid=(b,),
        in_specs=[pl.BlockSpec((1, t, d), lambda bi: (bi, 0, 0)), kv_spec, kv_spec],
        out_specs=pl.BlockSpec((1, t, d), lambda bi: (bi, 0, 0)),
        out_shape=jax.ShapeDtypeStruct((b, t, d), BF16),
        compiler_params=_cparams("parallel"),
        name="mem_attn",
    )(q, k, v)


def _mem_block_kernel(x_ref, g_ref, wq_ref, k_ref, v_ref, wo_ref, o_ref):
    x = x_ref[0]
    d = x.shape[1]
    hd = d // MEM_HEADS
    h = _rms(x, g_ref[...]).astype(BF16)
    heads = [slice(i * hd, (i + 1) * hd) for i in range(MEM_HEADS)]
    qs = [(_dot(h, wq_ref[:, sl]) * (hd ** -0.5)).astype(BF16) for sl in heads]
    ss = [_dot_nt(q, k_ref[0, :, sl].astype(BF16)) for q, sl in zip(qs, heads)]
    es = [jnp.exp(s - jnp.max(s, axis=-1, keepdims=True)) for s in ss]
    outs = [(_dot(e.astype(BF16), v_ref[0, :, sl].astype(BF16)) / jnp.sum(e, axis=-1, keepdims=True)).astype(BF16)
            for e, sl in zip(es, heads)]
    o_ref[0] = x + _dot(jnp.concatenate(outs, axis=1), wo_ref[...])


def _mem_block(x, g, wq, k, v, wo, tm):
    b, t, d = x.shape
    n = k.shape[1]
    return pl.pallas_call(
        _mem_block_kernel,
        grid=(b, t // tm),
        in_specs=[pl.BlockSpec((1, tm, d), lambda bi, i: (bi, i, 0)), _sel(*g), _sel(*wq),
                  pl.BlockSpec((1, n, d), lambda bi, i: (bi, 0, 0)),
                  pl.BlockSpec((1, n, d), lambda bi, i: (bi, 0, 0)), _sel(*wo)],
        out_specs=pl.BlockSpec((1, tm, d), lambda bi, i: (bi, i, 0)),
        out_shape=jax.ShapeDtypeStruct((b, t, d), F32),
        compiler_params=_cparams("parallel", "parallel"),
        name="mem_block",
    )(x, g[0], wq[0], k, v, wo[0])


def _ffn_kernel(x_ref, g_ref, wup_ref, cw_ref, wdn_ref, hist_ref, *rest, tm, halo_rows, step, final):
    if final:
        gf_ref, o_ref, st_ref, u_scr, halo, y_scr = rest
    else:
        o_ref, st_ref, u_scr, halo, y_scr = rest
    i = pl.program_id(1)
    dff = wdn_ref.shape[0]
    fc = FFN_CHUNK

    @pl.when(i == 0)
    def _():
        halo[...] = hist_ref[0]

    x = x_ref[0]
    h = _rms(x, g_ref[...]).astype(BF16)
    n_chunks = dff // fc
    o1 = halo_rows - step
    o2 = halo_rows - 2 * step

    def up(c):
        for half in range(2):
            cols = slice(half * dff + c * fc, half * dff + (c + 1) * fc)
            slot = 2 * (c % 2) + half
            u_scr[slot, 0:halo_rows, :] = halo[:, cols]
            u_scr[slot, halo_rows:halo_rows + tm, :] = _dot(h, wup_ref[:, cols])

    up(0)
    for c in range(n_chunks):
        if c + 1 < n_chunks:
            up(c + 1)
        conv = []
        for half in range(2):
            cols = slice(half * dff + c * fc, half * dff + (c + 1) * fc)
            slot = 2 * (c % 2) + half
            conv.append(cw_ref[2:3, cols] * u_scr[slot, halo_rows:halo_rows + tm, :]
                        + cw_ref[1:2, cols] * u_scr[slot, o1:o1 + tm, :]
                        + cw_ref[0:1, cols] * u_scr[slot, o2:o2 + tm, :])
            halo[:, cols] = u_scr[slot, tm:tm + halo_rows, :]
        y_scr[:, c * fc:(c + 1) * fc] = (conv[0] * _sigmoid(conv[0]) * conv[1]).astype(BF16)
    out = x + _dot(y_scr[...], wdn_ref[...])
    if final:
        out = _rms(out, gf_ref[...])
    o_ref[0] = out
    st_ref[0] = halo[...]


def _ffn(x, g, w_up, conv_w, w_down, hist, tm, step, g_final=None):
    b, t, d = x.shape
    dff = w_down[0].shape[-2]
    halo_rows = hist.shape[1]
    tm = min(tm, t)

    def const(arr, lead):
        nl = len(lead)
        shape = tuple(arr.shape[nl:])
        return pl.BlockSpec((None,) * nl + shape, lambda *_: tuple(lead) + (0,) * len(shape),
                            pipeline_mode=pl.Buffered(1))

    in_specs = [pl.BlockSpec((1, tm, d), lambda bi, i: (bi, i, 0)), const(*g), const(*w_up), const(*conv_w),
                const(*w_down), pl.BlockSpec((1, halo_rows, 2 * dff), lambda bi, i: (bi, 0, 0))]
    args = [x, g[0], w_up[0], conv_w[0], w_down[0], hist]
    if g_final is not None:
        in_specs.append(const(g_final, ()))
        args.append(g_final)
    return pl.pallas_call(
        functools.partial(_ffn_kernel, tm=tm, halo_rows=halo_rows, step=step, final=g_final is not None),
        grid=(b, t // tm),
        in_specs=in_specs,
        out_specs=[pl.BlockSpec((1, tm, d), lambda bi, i: (bi, i, 0)),
                   pl.BlockSpec((1, halo_rows, 2 * dff), lambda bi, i: (bi, 0, 0))],
        out_shape=[jax.ShapeDtypeStruct((b, t, d), F32), jax.ShapeDtypeStruct((b, halo_rows, 2 * dff), F32)],
        scratch_shapes=[pltpu.VMEM((4, halo_rows + tm, FFN_CHUNK), F32), pltpu.VMEM((halo_rows, 2 * dff), F32),
                        pltpu.VMEM((tm, dff), BF16)],
        compiler_params=_cparams("parallel", "arbitrary"),
        name="conv_ffn",
    )(*args)


def _pad_rows_top(a, rows):
    return jnp.pad(a, ((0, 0), (rows - a.shape[1], 0), (0, 0)))


def _row3(a):
    return a.reshape(a.shape[0], 1, a.shape[1])


def kernel(x_prompt, x_sample, state_pool, state_dn_conv, state_dn, cache_win_w128, cache_win_w512, cache_win_w2048, cache_mem_k, cache_mem_v, state_ffn_conv, mem_prompt, g_mix, w_in_ab, w_pool, pool_scale, dn_conv_w, dn_a_log, dn_dt_bias, dn_norm_w, w_out_ab, w_qkv_c, w_out_c, g_mem_q, g_mem_kv, w_mem_q, w_mem_k, w_mem_v, w_mem_o, g_ffn, w_up, ffn_conv_w, w_down, g_final):
    bp, s, d = x_prompt.shape
    bs, ts, _ = x_sample.shape
    depth = g_mix.shape[0]
    n_mem = mem_prompt.shape[1]
    dff = w_down.shape[1]
    d_pool = len(POOL_WINDOWS) * LANES
    dd = DN_HEADS * DN_DIM
    hd = d // MEM_HEADS
    win_caches = (cache_win_w128, cache_win_w512, cache_win_w2048)

    n_in = w_in_ab.shape[2]
    n_pad = -(-n_in // LANES) * LANES
    w_in_b = jnp.pad(w_in_ab, ((0, 0), (0, 0), (0, n_pad - n_in))).astype(BF16)
    w_pool_b, w_out_ab_b = w_pool.astype(BF16), w_out_ab.astype(BF16)
    w_qkv_b, w_out_c_b = w_qkv_c.astype(BF16), w_out_c.astype(BF16)
    wq_b, wk_b, wv_b, wo_b = (w.astype(BF16) for w in (w_mem_q, w_mem_k, w_mem_v, w_mem_o))
    w_up_b, w_down_b = w_up.astype(BF16), w_down.astype(BF16)
    g_mix3, g_mem_q3, g_mem_kv3, g_ffn3 = _row3(g_mix), _row3(g_mem_q), _row3(g_mem_kv), _row3(g_ffn)
    pool_scale3, norm_w3 = _row3(pool_scale), _row3(dn_norm_w)
    lane_pad = ((0, 0), (DN_HEADS, LANES - 2 * DN_HEADS))
    alog3, dtb3 = _row3(jnp.pad(dn_a_log, lane_pad)), _row3(jnp.pad(dn_dt_bias, lane_pad))
    g_final2 = g_final.reshape(1, d)

    xp = x_prompt.reshape(bp * s, d)
    xs = x_sample.reshape(bs * ts, d)
    mem_flat = mem_prompt.reshape(bp * n_mem, d)

    pool_p, pool_s, dconv_p, dconv_s, dn_p, dn_s = [], [], [], [], [], []
    win_p = [[] for _ in WIN_CONFIGS]
    win_s = [[] for _ in WIN_CONFIGS]
    memk_p, memv_p, fconv_p, fconv_s = [], [], [], []

    for layer in range(depth):
        if layer % 2 == 0:
            e = layer // 2
            zero_state = jnp.zeros((1, bp, DN_HEADS, DN_DIM, DN_DIM), F32)
            new_x = []
            for x, b, t, hist_pool, hist_conv, s0, p0 in (
                    (xp, bp, s, jnp.zeros((bp, POOL_HALO, d_pool), F32), jnp.zeros((bp, DN_HALO, 3 * dd), F32),
                     (zero_state, (0,)), 0),
                    (xs, bs, ts, _pad_rows_top(state_pool[e], POOL_HALO), _pad_rows_top(state_dn_conv[e], DN_HALO),
                     (state_dn, (e,)), PAST_LEN)):
                proj, = _norm_proj(x, (g_mix3, (layer,)), [(w_in_b, (e,))], [F32], tm=256)
                proj = proj.reshape(b, t, n_pad)
                o_dn, s_new = _delta(proj, hist_conv, s0, (dn_conv_w, (e,)), (alog3, (e,)), (dtb3, (e,)),
                                     (norm_w3, (e,)))
                pool_w = ((w_pool_b, (e,)), (pool_scale3, (e,)))
                if t >= 512:
                    x = _pool(proj, hist_pool, *pool_w, tm=512, p0=p0,
                              fuse=(o_dn, (w_out_ab_b, (e,)), x.reshape(b, t, d))).reshape(b * t, d)
                else:
                    y_pool = _pool(proj, hist_pool, *pool_w, tm=512, p0=p0)
                    x = _proj_res([y_pool.reshape(b * t, d_pool), o_dn.reshape(b * t, dd)],
                                  [(w_out_ab_b, (e,), 0), (w_out_ab_b, (e,), 1)], x, tm=512)
                new_x.append((x, proj, s_new))
            (xp, proj_p, s_p), (xs, proj_s, s_s) = new_x
            pool_p.append(proj_p[:, s - (POOL_HALO - 1):, :d_pool])
            pool_s.append(jnp.concatenate([state_pool[e], proj_s[:, :, :d_pool]], axis=1)[:, -(POOL_HALO - 1):])
            dconv_p.append(proj_p[:, s - (DN_CONV - 1):, d_pool:d_pool + 3 * dd])
            dconv_s.append(jnp.concatenate([state_dn_conv[e], proj_s[:, :, d_pool:d_pool + 3 * dd]],
                                           axis=1)[:, -(DN_CONV - 1):])
            dn_p.append(s_p)
            dn_s.append(s_s)
        else:
            o = layer // 2
            c = w_qkv_c.shape[2]
            qkv_p = _qkv_rope(xp, (g_mix3, (layer,)), (w_qkv_b, (o,)), _rope_tables(jnp.arange(s)), tm=256)
            tab_s = tuple(jnp.tile(u, (bs, 1)) for u in _rope_tables(PAST_LEN + jnp.arange(ts)))
            qkv_s = _qkv_rope(xs, (g_mix3, (layer,)), (w_qkv_b, (o,)), tab_s, tm=256)
            qkv_p3 = qkv_p.reshape(bp, s, c)
            qkv_s3 = qkv_s.reshape(bs, ts, c)
            xp = _proj_res([_swa_prompt(qkv_p3)], [(w_out_c_b, (o,), 0)], xp, tm=512)
            caches = [ch.transpose(0, 1, 3, 4, 5, 2).reshape(ch.shape[:2] + (2, D_SWA, ch.shape[2]))
                      for ch in win_caches]
            o_s = _swa_sample(qkv_s3, caches, o)
            xs = _proj_res([o_s.reshape(bs * ts, D_SWA)], [(w_out_c_b, (o,), 0)], xs, tm=512)
            for gi, (win, _) in enumerate(WIN_CONFIGS):
                base = gi * 3 * D_SWA
                keep = min(win, s)
                kv_p = qkv_p3[:, s - keep:, base + D_SWA:base + 3 * D_SWA]
                win_p[gi].append(kv_p.reshape(bp, keep, 2, SWA_HEADS, SWA_DIM))
                kv_s = qkv_s3[:, :, base + D_SWA:base + 3 * D_SWA]
                win_s[gi].append(kv_s.reshape(bs, ts, 2, SWA_HEADS, SWA_DIM))

        mk, mv = _norm_proj(mem_flat, (g_mem_kv3, (layer,)), [(wk_b, (layer,)), (wv_b, (layer,))], [F32, F32], tm=256)
        memk_p.append(mk.reshape(bp, n_mem, MEM_HEADS, hd))
        memv_p.append(mv.reshape(bp, n_mem, MEM_HEADS, hd))
        xp = _mem_block(xp.reshape(bp, s, d), (g_mem_q3, (layer,)), (wq_b, (layer,)), mk.reshape(bp, n_mem, d),
                        mv.reshape(bp, n_mem, d), (wo_b, (layer,)), tm=512).reshape(bp * s, d)
        q_s, = _norm_proj(xs, (g_mem_q3, (layer,)), [(wq_b, (layer,))], [BF16], tm=512)
        att_s = _mem_attn(q_s.reshape(bs, ts, d), cache_mem_k, cache_mem_v, layer)
        xs = _proj_res([att_s.reshape(bs * ts, d)], [(wo_b, (layer,), 0)], xs, tm=512)

        gf = g_final2 if layer == depth - 1 else None
        ffn_w = ((g_ffn3, (layer,)), (w_up_b, (layer,)), (ffn_conv_w, (layer,)), (w_down_b, (layer,)))
        yp, st_p = _ffn(xp.reshape(bp, s, d), *ffn_w, jnp.zeros((bp, 8, 2 * dff), F32), tm=512, step=1, g_final=gf)
        xp = yp.reshape(bp * s, d)
        fconv_p.append(st_p[:, -2:])
        xs_tm = xs.reshape(bs, ts, d).transpose(1, 0, 2).reshape(1, ts * bs, d)
        hist_tm = state_ffn_conv[layer].transpose(1, 0, 2).reshape(1, 2 * bs, 2 * dff)
        ys, st_s = _ffn(xs_tm, *ffn_w, hist_tm, tm=ts * bs, step=bs, g_final=gf)
        xs = ys.reshape(ts, bs, d).transpose(1, 0, 2).reshape(bs * ts, d)
        fconv_s.append(st_s.reshape(2, bs, 2 * dff).transpose(1, 0, 2))

    return (xp.reshape(bp, s, d), xs.reshape(bs, ts, d),
            jnp.stack(pool_p), jnp.stack(pool_s),
            jnp.stack(dconv_p), jnp.stack(dconv_s),
            jnp.stack(dn_p), jnp.stack(dn_s),
            jnp.stack(win_p[0]), jnp.stack(win_s[0]),
            jnp.stack(win_p[1]), jnp.stack(win_s[1]),
            jnp.stack(win_p[2]), jnp.stack(win_s[2]),
            jnp.stack(memk_p), jnp.stack(memv_p),
            jnp.stack(fconv_p), jnp.stack(fconv_s))
```

```python
import functools

import jax
import jax.numpy as jnp
from jax import lax
from jax.experimental import pallas as pl
from jax.experimental.pallas import tpu as pltpu

F32 = jnp.float32
BF16 = jnp.bfloat16
EPS = 1e-6

PAST_LEN = 8192
POOL_WINDOWS = (2, 4, 8, 16)
POOL_HALO = 16
DN_HEADS = 4
DN_DIM = 128
DN_CONV = 4
DN_CHUNK = 128
DN_HALO = 8
DN_BATCH = 4
WIN_CONFIGS = ((128, 1), (512, 4), (2048, 16))
SWA_HEADS = 8
SWA_DIM = 64
D_SWA = SWA_HEADS * SWA_DIM
SWA_BLOCK = 128
SWA_STAGE_BLOCKS = 16
ROT_HALF = 8
ROPE_THETA = 500000.0
LOG2_E = 1.4426950408889634
MEM_HEADS = 4
FFN_CHUNK = 1408
LANES = 128
VMEM_LIMIT_BYTES = 56 * 1024 * 1024


def _cparams(*sem):
    return pltpu.CompilerParams(dimension_semantics=sem, vmem_limit_bytes=VMEM_LIMIT_BYTES)


def _rms(x, g):
    return x * lax.rsqrt(jnp.mean(x * x, axis=-1, keepdims=True) + EPS) * g


def _sigmoid(x):
    return 1.0 / (1.0 + jnp.exp(-x))


def _dot(a, b):
    return jnp.dot(a, b, preferred_element_type=F32)


def _dot_nt(a, b):
    return lax.dot_general(a, b, (((1,), (1,)), ((), ())), preferred_element_type=F32)


def _dot_tn(a, b):
    return lax.dot_general(a, b, (((0,), (0,)), ((), ())), preferred_element_type=F32)


def _sel(arr, lead, block=None, index=None):
    nl = len(lead)
    shape = tuple(arr.shape[nl:]) if block is None else tuple(block)
    idx = (0,) * len(shape) if index is None else tuple(index)
    return pl.BlockSpec((None,) * nl + shape, lambda *_: tuple(lead) + idx)


def _norm_proj_kernel(x_ref, g_ref, *refs, n_w, tn):
    w_refs, o_refs = refs[:n_w], refs[n_w:]
    h = _rms(x_ref[...], g_ref[...]).astype(BF16)
    for w_ref, o_ref in zip(w_refs, o_refs):
        n = w_ref.shape[1]
        for j in range(0, n, tn):
            jw = min(tn, n - j)
            o_ref[:, j:j + jw] = _dot(h, w_ref[:, j:j + jw]).astype(o_ref.dtype)


def _norm_proj(x, g, ws, out_dtypes, tm):
    m, d = x.shape
    tm = min(tm, m)
    g_arr, g_lead = g
    in_specs = [pl.BlockSpec((tm, d), lambda i: (i, 0)), _sel(g_arr, g_lead)]
    in_specs += [_sel(w, lead) for w, lead in ws]
    widths = [w.shape[-1] for w, _ in ws]
    return pl.pallas_call(
        functools.partial(_norm_proj_kernel, n_w=len(ws), tn=512),
        grid=(m // tm,),
        in_specs=in_specs,
        out_specs=[pl.BlockSpec((tm, n), lambda i: (i, 0)) for n in widths],
        out_shape=[jax.ShapeDtypeStruct((m, n), dt) for n, dt in zip(widths, out_dtypes)],
        compiler_params=_cparams("parallel"),
        name="norm_proj",
    )(x, g_arr, *[w for w, _ in ws])


def _qkv_rope_kernel(x_ref, g_ref, w_ref, cos_ref, sa_ref, sb_ref, o_ref):
    h = _rms(x_ref[...], g_ref[...]).astype(BF16)
    cos, sa, sb = cos_ref[...], sa_ref[...], sb_ref[...]
    for piece in range(w_ref.shape[1] // D_SWA):
        c0 = piece * D_SWA
        y = _dot(h, w_ref[:, c0:c0 + D_SWA])
        if piece % 3 == 2:
            o_ref[:, c0:c0 + D_SWA] = y
            continue
        for a in range(D_SWA // LANES):
            ya = y[:, a * LANES:(a + 1) * LANES]
            rot = (ya * cos + pltpu.roll(ya, LANES - ROT_HALF, axis=1) * sa
                   + pltpu.roll(ya, ROT_HALF, axis=1) * sb)
            o_ref[:, c0 + a * LANES:c0 + (a + 1) * LANES] = rot


def _rope_tables(pos):
    t = pos.shape[0]
    inv_freq = ROPE_THETA ** (-jnp.arange(ROT_HALF, dtype=F32) / ROT_HALF)
    ang = pos.astype(F32)[:, None] * inv_freq[None, :]
    cos, sin = jnp.cos(ang), jnp.sin(ang)
    rest = SWA_DIM - 2 * ROT_HALF
    c64 = jnp.concatenate([cos, cos, jnp.ones((t, rest), F32)], axis=1)
    a64 = jnp.concatenate([-sin, jnp.zeros((t, SWA_DIM - ROT_HALF), F32)], axis=1)
    b64 = jnp.concatenate([jnp.zeros((t, ROT_HALF), F32), sin, jnp.zeros((t, rest), F32)], axis=1)
    return tuple(jnp.concatenate([u, u], axis=1) for u in (c64, a64, b64))


def _qkv_rope(x, g, w, tables, tm):
    m, d = x.shape
    w_arr, w_lead = w
    g_arr, g_lead = g
    n = w_arr.shape[-1]
    tm = min(tm, m)
    nper = tables[0].shape[0] // tm
    tab_spec = pl.BlockSpec((tm, LANES), lambda i: (i % nper, 0))
    return pl.pallas_call(
        _qkv_rope_kernel,
        grid=(m // tm,),
        in_specs=[pl.BlockSpec((tm, d), lambda i: (i, 0)), _sel(g_arr, g_lead), _sel(w_arr, w_lead),
                  tab_spec, tab_spec, tab_spec],
        out_specs=pl.BlockSpec((tm, n), lambda i: (i, 0)),
        out_shape=jax.ShapeDtypeStruct((m, n), F32),
        compiler_params=_cparams("parallel"),
        name="qkv_rope",
    )(x, g_arr, w_arr, *tables)


def _proj_res_kernel(*refs, n_in):
    a_refs, w_refs = refs[:n_in], refs[n_in:2 * n_in]
    x_ref, o_ref = refs[2 * n_in], refs[2 * n_in + 1]
    acc = x_ref[...]
    for a_ref, w_ref in zip(a_refs, w_refs):
        acc = acc + _dot(a_ref[...], w_ref[...])
    o_ref[...] = acc


def _proj_res(acts, ws, x, tm):
    m, d = x.shape
    tm = min(tm, m)
    in_specs = [pl.BlockSpec((tm, a.shape[1]), lambda i: (i, 0)) for a in acts]
    in_specs += [_sel(w, lead, block=(a.shape[1], d), index=(blk, 0)) for a, (w, lead, blk) in zip(acts, ws)]
    in_specs += [pl.BlockSpec((tm, d), lambda i: (i, 0))]
    return pl.pallas_call(
        functools.partial(_proj_res_kernel, n_in=len(acts)),
        grid=(m // tm,),
        in_specs=in_specs,
        out_specs=pl.BlockSpec((tm, d), lambda i: (i, 0)),
        out_shape=jax.ShapeDtypeStruct((m, d), F32),
        compiler_params=_cparams("parallel"),
        name="proj_res",
    )(*acts, *[w for w, _, _ in ws], x)


def _pool_kernel(a_ref, hist_ref, w_ref, sc_ref, *rest, tm, p0, fused):
    if fused:
        odn_ref, wa_ref, wb_ref, x_ref, o_ref, buf = rest
    else:
        o_ref, buf = rest
    i = pl.program_id(1)

    @pl.when(i == 0)
    def _():
        buf[0:POOL_HALO, :] = hist_ref[0]

    @pl.when(i > 0)
    def _():
        buf[0:POOL_HALO, :] = buf[tm:tm + POOL_HALO, :]

    buf[POOL_HALO:POOL_HALO + tm, :] = a_ref[0]
    pos = p0 + i * tm + lax.broadcasted_iota(jnp.int32, (tm, 1), 0)
    ys = []
    for gi, win in enumerate(POOL_WINDOWS):
        cs = slice(gi * LANES, (gi + 1) * LANES)
        cur = buf[POOL_HALO:POOL_HALO + tm, cs]
        tot = cur
        for j in range(1, win):
            tot = tot + buf[POOL_HALO - j:POOL_HALO - j + tm, cs]
        cnt = jnp.minimum(pos + 1, win).astype(F32)
        z = tot / cnt - cur
        ys.append((_dot(z.astype(BF16), w_ref[gi]) * sc_ref[:, cs]).astype(BF16))
    y = jnp.concatenate(ys, axis=1)
    if fused:
        o_ref[0] = x_ref[0] + _dot(y, wa_ref[...]) + _dot(odn_ref[0], wb_ref[...])
    else:
        o_ref[0] = y


def _pool(proj, hist, w, scale, tm, p0, fuse=None):
    b, t, _ = proj.shape
    dp = len(POOL_WINDOWS) * LANES
    tm = min(tm, t)
    in_specs = [pl.BlockSpec((1, tm, dp), lambda bi, i: (bi, i, 0)),
                pl.BlockSpec((1, POOL_HALO, dp), lambda bi, i: (bi, 0, 0)),
                _sel(*w), _sel(*scale)]
    args = [proj, hist, w[0], scale[0]]
    out_w, out_dt = dp, BF16
    if fuse is not None:
        o_dn, (w_out, lead), x = fuse
        dd, d = o_dn.shape[2], x.shape[2]
        in_specs += [pl.BlockSpec((1, tm, dd), lambda bi, i: (bi, i, 0)),
                     _sel(w_out, lead, block=(dp, d), index=(0, 0)),
                     _sel(w_out, lead, block=(dd, d), index=(dp // dd, 0)),
                     pl.BlockSpec((1, tm, d), lambda bi, i: (bi, i, 0))]
        args += [o_dn, w_out, w_out, x]
        out_w, out_dt = d, F32
    return pl.pallas_call(
        functools.partial(_pool_kernel, tm=tm, p0=p0, fused=fuse is not None),
        grid=(b, t // tm),
        in_specs=in_specs,
        out_specs=pl.BlockSpec((1, tm, out_w), lambda bi, i: (bi, i, 0)),
        out_shape=jax.ShapeDtypeStruct((b, t, out_w), out_dt),
        scratch_shapes=[pltpu.VMEM((POOL_HALO + tm, dp), F32)],
        compiler_params=_cparams("parallel", "arbitrary"),
        name="pool_mix",
    )(*args)


def _cumsum_rows(x):
    n = x.shape[0]
    row = lax.broadcasted_iota(jnp.int32, x.shape, 0)
    sh = 1
    while sh < n:
        x = x + jnp.where(row >= sh, pltpu.roll(x, sh, axis=0), 0.0)
        sh *= 2
    return x


def _delta_kernel(q_ref, k_ref, v_ref, gate_ref, ba_ref, hist_ref, s0_ref, cw_ref, alog_ref, dtb_ref,
                  nw_ref, o_ref, s_ref, ext, *, rows, nb):
    c = pl.program_id(1)
    C = DN_CHUNK
    dd = DN_HEADS * DN_DIM

    @pl.when(c == 0)
    def _():
        ext[:, 0:DN_HALO, :] = hist_ref[...]
        s_ref[...] = s0_ref[...]

    ri = lax.broadcasted_iota(jnp.int32, (C, C), 0)
    ci = lax.broadcasted_iota(jnp.int32, (C, C), 1)
    incl = ri >= ci
    strict = ri > ci
    eye = (ri == ci).astype(F32)
    valid = lax.broadcasted_iota(jnp.int32, (C, LANES), 0) < rows

    chains = []
    for bi in range(nb):
        ext[bi, DN_HALO:DN_HALO + rows, 0:dd] = q_ref[bi]
        ext[bi, DN_HALO:DN_HALO + rows, dd:2 * dd] = k_ref[bi]
        ext[bi, DN_HALO:DN_HALO + rows, 2 * dd:3 * dd] = v_ref[bi]
        if rows < C:
            ext[bi, DN_HALO + rows:DN_HALO + C, :] = jnp.zeros((C - rows, 3 * dd), F32)
        conv = cw_ref[DN_CONV - 1:DN_CONV, :] * ext[bi, DN_HALO:DN_HALO + C, :]
        for kk in range(DN_CONV - 1):
            off = DN_HALO - (DN_CONV - 1) + kk
            conv = conv + cw_ref[kk:kk + 1, :] * ext[bi, off:off + C, :]
        act = conv * _sigmoid(conv)
        ext[bi, 0:DN_HALO, :] = ext[bi, rows:rows + DN_HALO, :]

        ba = ba_ref[bi]
        gate = gate_ref[bi]
        if rows < C:
            ba = jnp.concatenate([ba, jnp.zeros((C - rows, LANES), F32)], axis=0)
        beta_t = jnp.where(valid, _sigmoid(ba), 0.0)
        xg = ba + dtb_ref[...]
        softplus = jnp.maximum(xg, 0.0) + jnp.log(1.0 + jnp.exp(-jnp.abs(xg)))
        g_t = jnp.where(valid, -jnp.exp(alog_ref[...]) * softplus, 0.0)
        gcum = _cumsum_rows(g_t)
        gcum_t = gcum.T
        e_g = jnp.exp(gcum)
        g_last = gcum[C - 1:C, :]
        e_rev = jnp.exp(g_last - gcum)
        e_last = jnp.exp(g_last)

        for h in range(DN_HEADS):
            hs = slice(h * DN_DIM, (h + 1) * DN_DIM)
            qh = act[:, hs]
            kh = act[:, dd + h * DN_DIM:dd + (h + 1) * DN_DIM]
            vh = act[:, 2 * dd + h * DN_DIM:2 * dd + (h + 1) * DN_DIM]
            qn = qh * lax.rsqrt(jnp.sum(qh * qh, axis=-1, keepdims=True) + EPS) * (DN_DIM ** -0.5)
            kn = kh * lax.rsqrt(jnp.sum(kh * kh, axis=-1, keepdims=True) + EPS)
            beta = beta_t[:, h:h + 1]
            gcol = gcum[:, DN_HEADS + h:DN_HEADS + h + 1]
            grow = gcum_t[DN_HEADS + h:DN_HEADS + h + 1, :]
            eg = e_g[:, DN_HEADS + h:DN_HEADS + h + 1]
            chains.append(dict(
                bi=bi, h=h, hs=hs, qn=qn, kn=kn, kb=kn.astype(BF16), beta=beta,
                decay=jnp.where(incl, jnp.exp(gcol - grow), 0.0),
                rhs_u=(vh * beta).astype(BF16), rhs_w=(kn * (beta * eg)).astype(BF16),
                qg=(qn * eg).astype(BF16), kg=(kn * e_rev[:, DN_HEADS + h:DN_HEADS + h + 1]).astype(BF16),
                el=e_last[:, DN_HEADS + h:DN_HEADS + h + 1], gate=gate[:, hs]))

    for ch in chains:
        ch["a"] = jnp.where(strict, _dot_nt(ch["kb"], ch["kb"]) * ch["decay"] * ch["beta"], 0.0)
        a_blk = jnp.where((ri >> 4) == (ci >> 4), ch["a"], 0.0)
        ch["p"] = eye - a_blk
        ch["xp"] = a_blk
    for _ in range(3):
        for ch in chains:
            xb = ch["xp"].astype(BF16)
            ch["xp"] = _dot(xb, xb)
        for ch in chains:
            ch["p"] = ch["p"] + _dot(ch["p"].astype(BF16), ch["xp"].astype(BF16))
    sh = 4
    while (1 << sh) < C and (1 << sh) < rows:
        off_mask =((ri >> (sh + 1)) == (ci >> (sh + 1))) & ((ri >> sh) != (ci >> sh))
        for ch in chains:
            ch["pb"] = ch["p"].astype(BF16)
            ch["t"] = _dot(ch["pb"], jnp.where(off_mask, ch["a"], 0.0).astype(BF16))
        for ch in chains:
            ch["p"] = ch["p"] - _dot(ch["t"].astype(BF16), ch["pb"])
        sh += 1
    for ch in chains:
        pb = ch["p"].astype(BF16)
        ch["u"] = _dot(pb, ch["rhs_u"])
        ch["w"] = _dot(pb, ch["rhs_w"])
        ch["qk"] = (_dot_nt(ch["qn"].astype(BF16), ch["kb"]) * ch["decay"]).astype(BF16)
    for ch in chains:
        ch["s"] = s_ref[ch["bi"], ch["h"]]
        ch["sb"] = ch["s"].astype(BF16)
        ch["vb"] = (ch["u"] - _dot(ch["w"].astype(BF16), ch["sb"])).astype(BF16)
    for ch in chains:
        ch["o"] = _dot(ch["qg"], ch["sb"]) + _dot(ch["qk"], ch["vb"])
        s_ref[ch["bi"], ch["h"]] = ch["s"] * ch["el"] + _dot_tn(ch["kg"], ch["vb"])
    for ch in chains:
        o = ch["o"]
        o = o * lax.rsqrt(jnp.mean(o * o, axis=-1, keepdims=True) + EPS) * nw_ref[...]
        o = o[0:rows] * (ch["gate"] * _sigmoid(ch["gate"]))
        o_ref[ch["bi"], :, ch["hs"]] = o.astype(o_ref.dtype)


def _delta(proj, hist, s0, conv_w, alog_row, dtb_row, norm_w):
    b, t, _ = proj.shape
    dd = DN_HEADS * DN_DIM
    rows = min(DN_CHUNK, t)
    nb = DN_BATCH
    s_arr, s_lead = s0
    col = lambda j: pl.BlockSpec((nb, rows, dd), lambda bi, c: (bi, c, j))
    st_block = (nb, DN_HEADS, DN_DIM, DN_DIM)
    return pl.pallas_call(
        functools.partial(_delta_kernel, rows=rows, nb=nb),
        grid=(b // nb, t // rows),
        in_specs=[col(1), col(2), col(3), col(4),
                  pl.BlockSpec((nb, rows, LANES), lambda bi, c: (bi, c, 5 * dd // LANES)),
                  pl.BlockSpec((nb, DN_HALO, 3 * dd), lambda bi, c: (bi, 0, 0)),
                  pl.BlockSpec((None,) * len(s_lead) + st_block, lambda bi, c: tuple(s_lead) + (bi, 0, 0, 0)),
                  _sel(*conv_w), _sel(*alog_row), _sel(*dtb_row), _sel(*norm_w)],
        out_specs=[pl.BlockSpec((nb, rows, dd), lambda bi, c: (bi, c, 0)),
                   pl.BlockSpec(st_block, lambda bi, c: (bi, 0, 0, 0))],
        out_shape=[jax.ShapeDtypeStruct((b, t, dd), BF16),
                   jax.ShapeDtypeStruct((b, DN_HEADS, DN_DIM, DN_DIM), F32)],
        scratch_shapes=[pltpu.VMEM((nb, DN_HALO + DN_CHUNK, 3 * dd), F32)],
        compiler_params=_cparams("parallel", "arbitrary"),
        name="delta_rule",
    )(proj, proj, proj, proj, proj, hist, s_arr, conv_w[0], alog_row[0], dtb_row[0], norm_w[0])


def _swa_group_bodies(n, q_ref, k_ref, v_ref, num_s, mx_s, den_s, gi, kst, vst, kpv, vpv, dil, nblk):
    blk = SWA_BLOCK
    span = blk * nblk

    def rows_of(r, first_blk, n_blk):
        if dil == 1:
            return pl.ds(first_blk * blk, n_blk * blk)
        return pl.ds(r + dil * blk * first_blk, n_blk * blk, stride=dil)

    @pl.when(n == 0)
    def _():
        kpv[...] = jnp.zeros_like(kpv)
        vpv[...] = jnp.zeros_like(vpv)

    for r in range(dil):
        kst[r] = k_ref[0, rows_of(r, 0, nblk), :].astype(BF16)
        vst[r] = v_ref[0, rows_of(r, 0, nblk), :].astype(BF16)

    qi = lax.broadcasted_iota(jnp.int32, (blk, blk), 0)
    kj = lax.broadcasted_iota(jnp.int32, (blk, blk), 1)
    below = kj <= qi
    above = kj >= qi
    above_first = above & (n > 0)
    lane = lax.broadcasted_iota(jnp.int32, (1, LANES), 1)
    zero = jnp.zeros((), BF16)
    scale = SWA_DIM ** -0.5 * LOG2_E

    blocks = [(r, j) for r in range(dil) for j in range(nblk)]
    for g0 in range(0, len(blocks), SWA_STAGE_BLOCKS):
        bodies = []
        for r, j in blocks[g0:g0 + SWA_STAGE_BLOCKS]:
            rs = slice(j * blk, (j + 1) * blk)
            q_b = (q_ref[0, rows_of(r, j, 1), :] * scale).astype(BF16)
            if j == 0:
                kp, vp, prev_ok = kpv[r], vpv[r], above_first
            else:
                ps = slice((j - 1) * blk, j * blk)
                kp, vp, prev_ok = kst[r, ps, :], vst[r, ps, :], above
            kc, vc = kst[r, rs, :], vst[r, rs, :]
            for half in range(2):
                hm = (lane < SWA_DIM) if half == 0 else (lane >= SWA_DIM)
                qh = jnp.where(hm, q_b, zero)
                bodies.append(dict(
                    r=r, j=j, half=half, hm=hm, vp=vp, vc=vc,
                    s_p=jnp.where(prev_ok, _dot_nt(qh, kp), -jnp.inf),
                    s_c=jnp.where(below, _dot_nt(qh, kc), -jnp.inf)))
        for bd in bodies:
            bd["m"] = jnp.max(jnp.maximum(bd["s_p"], bd["s_c"]), axis=-1, keepdims=True)
        for bd in bodies:
            e_p = jnp.exp2(bd["s_p"] - bd["m"])
            e_c = jnp.exp2(bd["s_c"] - bd["m"])
            bd["den"] = jnp.sum(e_p + e_c, axis=-1, keepdims=True)
            bd["e_p"], bd["e_c"] = e_p.astype(BF16), e_c.astype(BF16)
        for b0, b1 in zip(bodies[0::2], bodies[1::2]):
            acc = None
            for bd in (b0, b1):
                part = (_dot(bd["e_p"], jnp.where(bd["hm"], bd["vp"], zero))
                        + _dot(bd["e_c"], jnp.where(bd["hm"], bd["vc"], zero)))
                acc = part if acc is None else acc + part
            rows = rows_of(b0["r"], b0["j"], 1)
            num_s[gi, rows, :] = acc
            mx_s[gi, rows, :] = jnp.where(b0["hm"], b0["m"], b1["m"])
            den_s[gi, rows, :] = jnp.where(b0["hm"], b0["den"], b1["den"])

    for r in range(dil):
        kpv[r] = kst[r, span - blk:span, :]
        vpv[r] = vst[r, span - blk:span, :]


def _swa_prompt_kernel(*refs, dils, rows):
    ng = len(dils)
    in_refs, o_ref, scr = refs[:3 * ng], refs[3 * ng], refs[3 * ng + 1:]
    num_s, mx_s, den_s = scr[4 * ng:]
    n = pl.program_id(2)
    for gi, dil in enumerate(dils):
        _swa_group_bodies(n, *in_refs[3 * gi:3 * gi + 3], num_s, mx_s, den_s, gi, *scr[4 * gi:4 * gi + 4],
                          dil, rows // (SWA_BLOCK * dil))
    chunk = 2 * SWA_BLOCK
    for c0 in range(0, rows, chunk):
        rs = slice(c0, c0 + chunk)
        ms = [mx_s[gi, rs, :] for gi in range(ng)]
        mx = functools.reduce(jnp.maximum, ms)
        ws = [jnp.exp2(m - mx) for m in ms]
        tot = sum(w * den_s[gi, rs, :] for gi, w in enumerate(ws))
        o = sum((w / tot) * num_s[gi, rs, :] for gi, w in enumerate(ws))
        o_ref[0, rs, :] = o.astype(o_ref.dtype)


def _swa_prompt(qkv):
    b, s, _ = qkv.shape
    dils = tuple(dil for _, dil in WIN_CONFIGS)
    rows = SWA_BLOCK * max(dils)
    npair = D_SWA // LANES
    in_specs, scratch = [], []
    for gi, dil in enumerate(dils):
        for j in range(3):
            in_specs.append(pl.BlockSpec((1, rows, LANES),
                                         lambda bi, pp, n, gi=gi, j=j: (bi, n, (3 * gi + j) * npair + pp)))
        stage = pltpu.VMEM((dil, rows // dil, LANES), BF16)
        prev = pltpu.VMEM((dil, SWA_BLOCK, LANES), BF16)
        scratch += [stage, stage, prev, prev]
    scratch += [pltpu.VMEM((len(dils), rows, LANES), F32)] * 3
    out = pl.pallas_call(
        functools.partial(_swa_prompt_kernel, dils=dils, rows=rows),
        grid=(b, npair, s // rows),
        in_specs=in_specs,
        out_specs=pl.BlockSpec((1, rows, LANES), lambda bi, pp, n: (bi, n, pp)),
        out_shape=jax.ShapeDtypeStruct((b, s, D_SWA), BF16),
        scratch_shapes=scratch,
        compiler_params=_cparams("parallel", "parallel", "arbitrary"),
        name="swa_prompt",
    )(*([qkv] * (3 * len(dils))))
    return out.reshape(b * s, D_SWA)


def _swa_sample_kernel(qkv_ref, c0_ref, c1_ref, c2_ref, o_ref, *, t_new):
    caches = (c0_ref, c1_ref, c2_ref)
    rows = SWA_HEADS * t_new
    rh = jnp.concatenate([jnp.full((t_new, D_SWA), h, jnp.int32) for h in range(SWA_HEADS)], axis=0)
    chd = lax.broadcasted_iota(jnp.int32, (rows, D_SWA), 1) >> 6
    head_mask = rh == chd
    scale = SWA_DIM ** -0.5
    pad = jnp.zeros((LANES - t_new, D_SWA), BF16)
    pieces = []
    for gi, (win, dil) in enumerate(WIN_CONFIGS):
        c_ref = caches[gi]
        n_hist = c_ref.shape[2]
        base = gi * 3 * D_SWA
        q = qkv_ref[0, :, base:base + D_SWA] * scale
        qrows = jnp.where(head_mask, jnp.concatenate([q] * SWA_HEADS, axis=0), 0.0).astype(BF16)
        tok = jnp.concatenate([lax.broadcasted_iota(jnp.int32, (t_new, n_hist), 0)] * SWA_HEADS, axis=0)
        dist = n_hist + tok - lax.broadcasted_iota(jnp.int32, (rows, n_hist), 1)
        ok = (dist <= win) & ((dist & (dil - 1)) == 0)
        s_c = jnp.where(ok, _dot(qrows, c_ref[0].astype(BF16)), -jnp.inf)
        k_new = jnp.concatenate([qkv_ref[0, :, base + D_SWA:base + 2 * D_SWA].astype(BF16), pad], axis=0)
        v_new = jnp.concatenate([qkv_ref[0, :, base + 2 * D_SWA:base + 3 * D_SWA].astype(BF16), pad], axis=0)
        tok_n = jnp.concatenate([lax.broadcasted_iota(jnp.int32, (t_new, LANES), 0)] * SWA_HEADS, axis=0)
        dn = tok_n - lax.broadcasted_iota(jnp.int32, (rows, LANES), 1)
        ok_n = (dn >= 0) & (dn <= win) & ((dn & (dil - 1)) == 0)
        s_n = jnp.where(ok_n, _dot_nt(qrows, k_new), -jnp.inf)
        pieces.append((s_c, s_n, c_ref, v_new))
    m = None
    for s_c, s_n, _, _ in pieces:
        mm = jnp.maximum(jnp.max(s_c, axis=-1, keepdims=True), jnp.max(s_n, axis=-1, keepdims=True))
        m = mm if m is None else jnp.maximum(m, mm)
    den = jnp.zeros((rows, 1), F32)
    num = jnp.zeros((rows, D_SWA), F32)
    for s_c, s_n, c_ref, v_new in pieces:
        e_c = jnp.exp(s_c - m)
        e_n = jnp.exp(s_n - m)
        den = den + jnp.sum(e_c, axis=-1, keepdims=True) + jnp.sum(e_n, axis=-1, keepdims=True)
        num = num + _dot_nt(e_c.astype(BF16), c_ref[1].astype(BF16)) + _dot(e_n.astype(BF16), v_new)
    o = jnp.where(head_mask, num / den, 0.0)
    out = o[0:t_new]
    for h in range(1, SWA_HEADS):
        out = out + o[h * t_new:(h + 1) * t_new]
    o_ref[0] = out.astype(o_ref.dtype)


def _swa_sample(qkv, caches, layer_idx):
    b, t, c = qkv.shape
    return pl.pallas_call(
        functools.partial(_swa_sample_kernel, t_new=t),
        grid=(b,),
        in_specs=[pl.BlockSpec((1, t, c), lambda bi: (bi, 0, 0))]
        + [pl.BlockSpec((None, None) + ch.shape[2:], lambda bi: (layer_idx, bi, 0, 0, 0)) for ch in caches],
        out_specs=pl.BlockSpec((1, t, D_SWA), lambda bi: (bi, 0, 0)),
        out_shape=jax.ShapeDtypeStruct((b, t, D_SWA), BF16),
        compiler_params=_cparams("parallel"),
        name="swa_sample",
    )(qkv, *caches)


def _mem_attn_kernel(q_ref, k_ref, v_ref, o_ref):
    n, nh, hd = k_ref.shape
    t = q_ref.shape[1]
    q = q_ref[0]
    qs = jnp.concatenate([q[:, h * hd:(h + 1) * hd] for h in range(nh)], axis=0)
    k_all = k_ref[...].reshape(n * nh, hd).astype(BF16)
    v_all = v_ref[...].reshape(n * nh, hd).astype(BF16)
    row_head = jnp.concatenate([jnp.full((t, n * nh), h, jnp.int32) for h in range(nh)], axis=0)
    col_head = lax.broadcasted_iota(jnp.int32, (nh * t, n * nh), 1) & (nh - 1)
    s = jnp.where(row_head == col_head, _dot_nt(qs, k_all) * (hd ** -0.5), -jnp.inf)
    e = jnp.exp(s - jnp.max(s, axis=-1, keepdims=True))
    o = _dot(e.astype(BF16), v_all) / jnp.sum(e, axis=-1, keepdims=True)
    o_ref[0] = jnp.concatenate([o[h * t:(h + 1) * t] for h in range(nh)], axis=1).astype(o_ref.dtype)


def _mem_attn(q, k, v, layer_idx):
    b, t, d = q.shape
    kv_spec = pl.BlockSpec((None, None) + k.shape[2:], lambda bi: (layer_idx, bi, 0, 0, 0))
    return pl.pallas_call(
        _mem_attn_kernel,
        grid=(b,),
        in_specs=[pl.BlockSpec((1, t, d), lambda bi: (bi, 0, 0)), kv_spec, kv_spec],
        out_specs=pl.BlockSpec((1, t, d), lambda bi: (bi, 0, 0)),
        out_shape=jax.ShapeDtypeStruct((b, t, d), BF16),
        compiler_params=_cparams("parallel"),
        name="mem_attn",
    )(q, k, v)


def _mem_block_kernel(x_ref, g_ref, wq_ref, k_ref, v_ref, wo_ref, o_ref):
    x = x_ref[0]
    d = x.shape[1]
    hd = d // MEM_HEADS
    h = _rms(x, g_ref[...]).astype(BF16)
    heads = [slice(i * hd, (i + 1) * hd) for i in range(MEM_HEADS)]
    qs = [(_dot(h, wq_ref[:, sl]) * (hd ** -0.5)).astype(BF16) for sl in heads]
    ss = [_dot_nt(q, k_ref[0, :, sl].astype(BF16)) for q, sl in zip(qs, heads)]
    es = [jnp.exp(s - jnp.max(s, axis=-1, keepdims=True)) for s in ss]
    outs = [(_dot(e.astype(BF16), v_ref[0, :, sl].astype(BF16)) / jnp.sum(e, axis=-1, keepdims=True)).astype(BF16)
            for e, sl in zip(es, heads)]
    o_ref[0] = x + _dot(jnp.concatenate(outs, axis=1), wo_ref[...])


def _mem_block(x, g, wq, k, v, wo, tm):
    b, t, d = x.shape
    n = k.shape[1]
    return pl.pallas_call(
        _mem_block_kernel,
        grid=(b, t // tm),
        in_specs=[pl.BlockSpec((1, tm, d), lambda bi, i: (bi, i, 0)), _sel(*g), _sel(*wq),
                  pl.BlockSpec((1, n, d), lambda bi, i: (bi, 0, 0)),
                  pl.BlockSpec((1, n, d), lambda bi, i: (bi, 0, 0)), _sel(*wo)],
        out_specs=pl.BlockSpec((1, tm, d), lambda bi, i: (bi, i, 0)),
        out_shape=jax.ShapeDtypeStruct((b, t, d), F32),
        compiler_params=_cparams("parallel", "parallel"),
        name="mem_block",
    )(x, g[0], wq[0], k, v, wo[0])


def _ffn_kernel(x_ref, g_ref, wup_ref, cw_ref, wdn_ref, hist_ref, *rest, tm, halo_rows, step, final):
    if final:
        gf_ref, o_ref, st_ref, u_scr, halo, y_scr = rest
    else:
        o_ref, st_ref, u_scr, halo, y_scr = rest
    i = pl.program_id(1)
    dff = wdn_ref.shape[0]
    fc = FFN_CHUNK

    @pl.when(i == 0)
    def _():
        halo[...] = hist_ref[0]

    x = x_ref[0]
    h = _rms(x, g_ref[...]).astype(BF16)
    n_chunks = dff // fc
    o1 = halo_rows - step
    o2 = halo_rows - 2 * step

    def up(c):
        for half in range(2):
            cols = slice(half * dff + c * fc, half * dff + (c + 1) * fc)
            slot = 2 * (c % 2) + half
            u_scr[slot, 0:halo_rows, :] = halo[:, cols]
            u_scr[slot, halo_rows:halo_rows + tm, :] = _dot(h, wup_ref[:, cols])

    up(0)
    for c in range(n_chunks):
        if c + 1 < n_chunks:
            up(c + 1)
        conv = []
        for half in range(2):
            cols = slice(half * dff + c * fc, half * dff + (c + 1) * fc)
            slot = 2 * (c % 2) + half
            conv.append(cw_ref[2:3, cols] * u_scr[slot, halo_rows:halo_rows + tm, :]
                        + cw_ref[1:2, cols] * u_scr[slot, o1:o1 + tm, :]
                        + cw_ref[0:1, cols] * u_scr[slot, o2:o2 + tm, :])
            halo[:, cols] = u_scr[slot, tm:tm + halo_rows, :]
        y_scr[:, c * fc:(c + 1) * fc] = (conv[0] * _sigmoid(conv[0]) * conv[1]).astype(BF16)
    out = x + _dot(y_scr[...], wdn_ref[...])
    if final:
        out = _rms(out, gf_ref[...])
    o_ref[0] = out
    st_ref[0] = halo[...]


def _ffn(x, g, w_up, conv_w, w_down, hist, tm, step, g_final=None):
    b, t, d = x.shape
    dff = w_down[0].shape[-2]
    halo_rows = hist.shape[1]
    tm = min(tm, t)

    def const(arr, lead):
        nl = len(lead)
        shape = tuple(arr.shape[nl:])
        return pl.BlockSpec((None,) * nl + shape, lambda *_: tuple(lead) + (0,) * len(shape),
                            pipeline_mode=pl.Buffered(1))

    in_specs = [pl.BlockSpec((1, tm, d), lambda bi, i: (bi, i, 0)), const(*g), const(*w_up), const(*conv_w),
                const(*w_down), pl.BlockSpec((1, halo_rows, 2 * dff), lambda bi, i: (bi, 0, 0))]
    args = [x, g[0], w_up[0], conv_w[0], w_down[0], hist]
    if g_final is not None:
        in_specs.append(const(g_final, ()))
        args.append(g_final)
    return pl.pallas_call(
        functools.partial(_ffn_kernel, tm=tm, halo_rows=halo_rows, step=step, final=g_final is not None),
        grid=(b, t // tm),
        in_specs=in_specs,
        out_specs=[pl.BlockSpec((1, tm, d), lambda bi, i: (bi, i, 0)),
                   pl.BlockSpec((1, halo_rows, 2 * dff), lambda bi, i: (bi, 0, 0))],
        out_shape=[jax.ShapeDtypeStruct((b, t, d), F32), jax.ShapeDtypeStruct((b, halo_rows, 2 * dff), F32)],
        scratch_shapes=[pltpu.VMEM((4, halo_rows + tm, FFN_CHUNK), F32), pltpu.VMEM((halo_rows, 2 * dff), F32),
                        pltpu.VMEM((tm, dff), BF16)],
        compiler_params=_cparams("parallel", "arbitrary"),
        name="conv_ffn",
    )(*args)


def _pad_rows_top(a, rows):
    return jnp.pad(a, ((0, 0), (rows - a.shape[1], 0), (0, 0)))


def _row3(a):
    return a.reshape(a.shape[0], 1, a.shape[1])


def kernel(x_prompt, x_sample, state_pool, state_dn_conv, state_dn, cache_win_w128, cache_win_w512, cache_win_w2048, cache_mem_k, cache_mem_v, state_ffn_conv, mem_prompt, g_mix, w_in_ab, w_pool, pool_scale, dn_conv_w, dn_a_log, dn_dt_bias, dn_norm_w, w_out_ab, w_qkv_c, w_out_c, g_mem_q, g_mem_kv, w_mem_q, w_mem_k, w_mem_v, w_mem_o, g_ffn, w_up, ffn_conv_w, w_down, g_final):
    bp, s, d = x_prompt.shape
    bs, ts, _ = x_sample.shape
    depth = g_mix.shape[0]
    n_mem = mem_prompt.shape[1]
    dff = w_down.shape[1]
    d_pool = len(POOL_WINDOWS) * LANES
    dd = DN_HEADS * DN_DIM
    hd = d // MEM_HEADS
    win_caches = (cache_win_w128, cache_win_w512, cache_win_w2048)

    n_in = w_in_ab.shape[2]
    n_pad = -(-n_in // LANES) * LANES
    w_in_b = jnp.pad(w_in_ab, ((0, 0), (0, 0), (0, n_pad - n_in))).astype(BF16)
    w_pool_b, w_out_ab_b = w_pool.astype(BF16), w_out_ab.astype(BF16)
    w_qkv_b, w_out_c_b = w_qkv_c.astype(BF16), w_out_c.astype(BF16)
    wq_b, wk_b, wv_b, wo_b = (w.astype(BF16) for w in (w_mem_q, w_mem_k, w_mem_v, w_mem_o))
    w_up_b, w_down_b = w_up.astype(BF16), w_down.astype(BF16)
    g_mix3, g_mem_q3, g_mem_kv3, g_ffn3 = _row3(g_mix), _row3(g_mem_q), _row3(g_mem_kv), _row3(g_ffn)
    pool_scale3, norm_w3 = _row3(pool_scale), _row3(dn_norm_w)
    lane_pad = ((0, 0), (DN_HEADS, LANES - 2 * DN_HEADS))
    alog3, dtb3 = _row3(jnp.pad(dn_a_log, lane_pad)), _row3(jnp.pad(dn_dt_bias, lane_pad))
    g_final2 = g_final.reshape(1, d)

    xp = x_prompt.reshape(bp * s, d)
    xs = x_sample.reshape(bs * ts, d)
    mem_flat = mem_prompt.reshape(bp * n_mem, d)

    pool_p, pool_s, dconv_p, dconv_s, dn_p, dn_s = [], [], [], [], [], []
    win_p = [[] for _ in WIN_CONFIGS]
    win_s = [[] for _ in WIN_CONFIGS]
    memk_p, memv_p, fconv_p, fconv_s = [], [], [], []

    for layer in range(depth):
        if layer % 2 == 0:
            e = layer // 2
            zero_state = jnp.zeros((1, bp, DN_HEADS, DN_DIM, DN_DIM), F32)
            new_x = []
            for x, b, t, hist_pool, hist_conv, s0, p0 in (
                    (xp, bp, s, jnp.zeros((bp, POOL_HALO, d_pool), F32), jnp.zeros((bp, DN_HALO, 3 * dd), F32),
                     (zero_state, (0,)), 0),
                    (xs, bs, ts, _pad_rows_top(state_pool[e], POOL_HALO), _pad_rows_top(state_dn_conv[e], DN_HALO),
                     (state_dn, (e,)), PAST_LEN)):
                proj, = _norm_proj(x, (g_mix3, (layer,)), [(w_in_b, (e,))], [F32], tm=512)
                proj = proj.reshape(b, t, n_pad)
                o_dn, s_new = _delta(proj, hist_conv, s0, (dn_conv_w, (e,)), (alog3, (e,)), (dtb3, (e,)),
                                     (norm_w3, (e,)))
                pool_w = ((w_pool_b, (e,)), (pool_scale3, (e,)))
                if t >= 512:
                    x = _pool(proj, hist_pool, *pool_w, tm=512, p0=p0,
                              fuse=(o_dn, (w_out_ab_b, (e,)), x.reshape(b, t, d))).reshape(b * t, d)
                else:
                    y_pool = _pool(proj, hist_pool, *pool_w, tm=512, p0=p0)
                    x = _proj_res([y_pool.reshape(b * t, d_pool), o_dn.reshape(b * t, dd)],
                                  [(w_out_ab_b, (e,), 0), (w_out_ab_b, (e,), 1)], x, tm=512)
                new_x.append((x, proj, s_new))
            (xp, proj_p, s_p), (xs, proj_s, s_s) = new_x
            pool_p.append(proj_p[:, s - (POOL_HALO - 1):, :d_pool])
            pool_s.append(jnp.concatenate([state_pool[e], proj_s[:, :, :d_pool]], axis=1)[:, -(POOL_HALO - 1):])
            dconv_p.append(proj_p[:, s - (DN_CONV - 1):, d_pool:d_pool + 3 * dd])
            dconv_s.append(jnp.concatenate([state_dn_conv[e], proj_s[:, :, d_pool:d_pool + 3 * dd]],
                                           axis=1)[:, -(DN_CONV - 1):])
            dn_p.append(s_p)
            dn_s.append(s_s)
        else:
            o = layer // 2
            c = w_qkv_c.shape[2]
            qkv_p = _qkv_rope(xp, (g_mix3, (layer,)), (w_qkv_b, (o,)), _rope_tables(jnp.arange(s)), tm=512)
            tab_s = tuple(jnp.tile(u, (bs, 1)) for u in _rope_tables(PAST_LEN + jnp.arange(ts)))
            qkv_s = _qkv_rope(xs, (g_mix3, (layer,)), (w_qkv_b, (o,)), tab_s, tm=256)
            qkv_p3 = qkv_p.reshape(bp, s, c)
            qkv_s3 = qkv_s.reshape(bs, ts, c)
            xp = _proj_res([_swa_prompt(qkv_p3)], [(w_out_c_b, (o,), 0)], xp, tm=512)
            caches = [ch.transpose(0, 1, 3, 4, 5, 2).reshape(ch.shape[:2] + (2, D_SWA, ch.shape[2]))
                      for ch in win_caches]
            o_s = _swa_sample(qkv_s3, caches, o)
            xs = _proj_res([o_s.reshape(bs * ts, D_SWA)], [(w_out_c_b, (o,), 0)], xs, tm=512)
            for gi, (win, _) in enumerate(WIN_CONFIGS):
                base = gi * 3 * D_SWA
                keep = min(win, s)
                kv_p = qkv_p3[:, s - keep:, base + D_SWA:base + 3 * D_SWA]
                win_p[gi].append(kv_p.reshape(bp, keep, 2, SWA_HEADS, SWA_DIM))
                kv_s = qkv_s3[:, :, base + D_SWA:base + 3 * D_SWA]
                win_s[gi].append(kv_s.reshape(bs, ts, 2, SWA_HEADS, SWA_DIM))

        mk, mv = _norm_proj(mem_flat, (g_mem_kv3, (layer,)), [(wk_b, (layer,)), (wv_b, (layer,))], [F32, F32], tm=256)
        memk_p.append(mk.reshape(bp, n_mem, MEM_HEADS, hd))
        memv_p.append(mv.reshape(bp, n_mem, MEM_HEADS, hd))
        xp = _mem_block(xp.reshape(bp, s, d), (g_mem_q3, (layer,)), (wq_b, (layer,)), mk.reshape(bp, n_mem, d),
                        mv.reshape(bp, n_mem, d), (wo_b, (layer,)), tm=512).reshape(bp * s, d)
        q_s, = _norm_proj(xs, (g_mem_q3, (layer,)), [(wq_b, (layer,))], [BF16], tm=512)
        att_s = _mem_attn(q_s.reshape(bs, ts, d), cache_mem_k, cache_mem_v, layer)
        xs = _proj_res([att_s.reshape(bs * ts, d)], [(wo_b, (layer,), 0)], xs, tm=512)

        gf = g_final2 if layer == depth - 1 else None
        ffn_w = ((g_ffn3, (layer,)), (w_up_b, (layer,)), (ffn_conv_w, (layer,)), (w_down_b, (layer,)))
        yp, st_p = _ffn(xp.reshape(bp, s, d), *ffn_w, jnp.zeros((bp, 8, 2 * dff), F32), tm=512, step=1, g_final=gf)
        xp = yp.reshape(bp * s, d)
        fconv_p.append(st_p[:, -2:])
        xs_tm = xs.reshape(bs, ts, d).transpose(1, 0, 2).reshape(1, ts * bs, d)
        hist_tm = state_ffn_conv[layer].transpose(1, 0, 2).reshape(1, 2 * bs, 2 * dff)
        ys, st_s = _ffn(xs_tm, *ffn_w, hist_tm, tm=ts * bs, step=bs, g_final=gf)
        xs = ys.reshape(ts, bs, d).transpose(1, 0, 2).reshape(bs * ts, d)
        fconv_s.append(st_s.reshape(2, bs, 2 * dff).transpose(1, 0, 2))

    return (xp.reshape(bp, s, d), xs.reshape(bs, ts, d),
            jnp.stack(pool_p), jnp.stack(pool_s),
            jnp.stack(dconv_p), jnp.stack(dconv_s),
            jnp.stack(dn_p), jnp.stack(dn_s),
            jnp.stack(win_p[0]), jnp.stack(win_s[0]),
            jnp.stack(win_p[1]), jnp.stack(win_s[1]),
            jnp.stack(win_p[2]), jnp.stack(win_s[2]),
            jnp.stack(memk_p), jnp.stack(memv_p),
            jnp.stack(fconv_p), jnp.stack(fconv_s))
```

```python
import functools

import jax
import jax.numpy as jnp
from jax import lax
from jax.experimental import pallas as pl
from jax.experimental.pallas import tpu as pltpu

F32 = jnp.float32
BF16 = jnp.bfloat16
EPS = 1e-6

PAST_LEN = 8192
POOL_WINDOWS = (2, 4, 8, 16)
POOL_HALO = 16
DN_HEADS = 4
DN_DIM = 128
DN_CONV = 4
DN_CHUNK = 128
DN_HALO = 8
DN_BATCH = 4
DN_STAGE_CHAINS = 8
WIN_CONFIGS = ((128, 1), (512, 4), (2048, 16))
SWA_HEADS = 8
SWA_DIM = 64
D_SWA = SWA_HEADS * SWA_DIM
SWA_BLOCK = 128
SWA_STAGE_BLOCKS = 16
ROT_HALF = 8
ROPE_THETA = 500000.0
LOG2_E = 1.4426950408889634
MEM_HEADS = 4
FFN_CHUNK = 256
LANES = 128
VMEM_LIMIT_BYTES = 56 * 1024 * 1024


def _cparams(*sem):
    return pltpu.CompilerParams(dimension_semantics=sem, vmem_limit_bytes=VMEM_LIMIT_BYTES)


def _rms(x, g):
    return x * lax.rsqrt(jnp.mean(x * x, axis=-1, keepdims=True) + EPS) * g


def _sigmoid(x):
    return 1.0 / (1.0 + jnp.exp(-x))


def _dot(a, b):
    return jnp.dot(a, b, preferred_element_type=F32)


def _dot_nt(a, b):
    return lax.dot_general(a, b, (((1,), (1,)), ((), ())), preferred_element_type=F32)


def _dot_tn(a, b):
    return lax.dot_general(a, b, (((0,), (0,)), ((), ())), preferred_element_type=F32)


def _sel(arr, lead, block=None, index=None):
    nl = len(lead)
    shape = tuple(arr.shape[nl:]) if block is None else tuple(block)
    idx = (0,) * len(shape) if index is None else tuple(index)
    return pl.BlockSpec((None,) * nl + shape, lambda *_: tuple(lead) + idx)


def _norm_proj_kernel(x_ref, g_ref, *refs, n_w, tn):
    w_refs, o_refs = refs[:n_w], refs[n_w:]
    h = _rms(x_ref[...], g_ref[...]).astype(BF16)
    for w_ref, o_ref in zip(w_refs, o_refs):
        n = w_ref.shape[1]
        for j in range(0, n, tn):
            jw = min(tn, n - j)
            o_ref[:, j:j + jw] = _dot(h, w_ref[:, j:j + jw]).astype(o_ref.dtype)


def _norm_proj(x, g, ws, out_dtypes, tm):
    m, d = x.shape
    tm = min(tm, m)
    g_arr, g_lead = g
    in_specs = [pl.BlockSpec((tm, d), lambda i: (i, 0)), _sel(g_arr, g_lead)]
    in_specs += [_sel(w, lead) for w, lead in ws]
    widths = [w.shape[-1] for w, _ in ws]
    return pl.pallas_call(
        functools.partial(_norm_proj_kernel, n_w=len(ws), tn=512),
        grid=(m // tm,),
        in_specs=in_specs,
        out_specs=[pl.BlockSpec((tm, n), lambda i: (i, 0)) for n in widths],
        out_shape=[jax.ShapeDtypeStruct((m, n), dt) for n, dt in zip(widths, out_dtypes)],
        compiler_params=_cparams("parallel"),
        name="norm_proj",
    )(x, g_arr, *[w for w, _ in ws])


def _qkv_rope_kernel(x_ref, g_ref, w_ref, cos_ref, sa_ref, sb_ref, o_ref):
    h = _rms(x_ref[...], g_ref[...]).astype(BF16)
    cos, sa, sb = cos_ref[...], sa_ref[...], sb_ref[...]
    for piece in range(w_ref.shape[1] // D_SWA):
        c0 = piece * D_SWA
        y = _dot(h, w_ref[:, c0:c0 + D_SWA])
        if piece % 3 == 2:
            o_ref[:, c0:c0 + D_SWA] = y
            continue
        for a in range(D_SWA // LANES):
            ya = y[:, a * LANES:(a + 1) * LANES]
            rot = (ya * cos + pltpu.roll(ya, LANES - ROT_HALF, axis=1) * sa
                   + pltpu.roll(ya, ROT_HALF, axis=1) * sb)
            o_ref[:, c0 + a * LANES:c0 + (a + 1) * LANES] = rot


def _rope_tables(pos):
    t = pos.shape[0]
    inv_freq = ROPE_THETA ** (-jnp.arange(ROT_HALF, dtype=F32) / ROT_HALF)
    ang = pos.astype(F32)[:, None] * inv_freq[None, :]
    cos, sin = jnp.cos(ang), jnp.sin(ang)
    rest = SWA_DIM - 2 * ROT_HALF
    c64 = jnp.concatenate([cos, cos, jnp.ones((t, rest), F32)], axis=1)
    a64 = jnp.concatenate([-sin, jnp.zeros((t, SWA_DIM - ROT_HALF), F32)], axis=1)
    b64 = jnp.concatenate([jnp.zeros((t, ROT_HALF), F32), sin, jnp.zeros((t, rest), F32)], axis=1)
    return tuple(jnp.concatenate([u, u], axis=1) for u in (c64, a64, b64))


def _qkv_rope(x, g, w, tables, tm):
    m, d = x.shape
    w_arr, w_lead = w
    g_arr, g_lead = g
    n = w_arr.shape[-1]
    tm = min(tm, m)
    nper = tables[0].shape[0] // tm
    tab_spec = pl.BlockSpec((tm, LANES), lambda i: (i % nper, 0))
    return pl.pallas_call(
        _qkv_rope_kernel,
        grid=(m // tm,),
        in_specs=[pl.BlockSpec((tm, d), lambda i: (i, 0)), _sel(g_arr, g_lead), _sel(w_arr, w_lead),
                  tab_spec, tab_spec, tab_spec],
        out_specs=pl.BlockSpec((tm, n), lambda i: (i, 0)),
        out_shape=jax.ShapeDtypeStruct((m, n), F32),
        compiler_params=_cparams("parallel"),
        name="qkv_rope",
    )(x, g_arr, w_arr, *tables)


def _proj_res_kernel(*refs, n_in):
    a_refs, w_refs = refs[:n_in], refs[n_in:2 * n_in]
    x_ref, o_ref = refs[2 * n_in], refs[2 * n_in + 1]
    acc = x_ref[...]
    for a_ref, w_ref in zip(a_refs, w_refs):
        acc = acc + _dot(a_ref[...], w_ref[...])
    o_ref[...] = acc


def _proj_res(acts, ws, x, tm):
    m, d = x.shape
    tm = min(tm, m)
    in_specs = [pl.BlockSpec((tm, a.shape[1]), lambda i: (i, 0)) for a in acts]
    in_specs += [_sel(w, lead, block=(a.shape[1], d), index=(blk, 0)) for a, (w, lead, blk) in zip(acts, ws)]
    in_specs += [pl.BlockSpec((tm, d), lambda i: (i, 0))]
    return pl.pallas_call(
        functools.partial(_proj_res_kernel, n_in=len(acts)),
        grid=(m // tm,),
        in_specs=in_specs,
        out_specs=pl.BlockSpec((tm, d), lambda i: (i, 0)),
        out_shape=jax.ShapeDtypeStruct((m, d), F32),
        compiler_params=_cparams("parallel"),
        name="proj_res",
    )(*acts, *[w for w, _, _ in ws], x)


def _pool_kernel(a_ref, hist_ref, w_ref, sc_ref, *rest, tm, p0, fused):
    if fused:
        odn_ref, wa_ref, wb_ref, x_ref, o_ref, buf = rest
    else:
        o_ref, buf = rest
    i = pl.program_id(1)

    @pl.when(i == 0)
    def _():
        buf[0:POOL_HALO, :] = hist_ref[0]

    @pl.when(i > 0)
    def _():
        buf[0:POOL_HALO, :] = buf[tm:tm + POOL_HALO, :]

    buf[POOL_HALO:POOL_HALO + tm, :] = a_ref[0]
    pos = p0 + i * tm + lax.broadcasted_iota(jnp.int32, (tm, 1), 0)
    ys = []
    for gi, win in enumerate(POOL_WINDOWS):
        cs = slice(gi * LANES, (gi + 1) * LANES)
        cur = buf[POOL_HALO:POOL_HALO + tm, cs]
        tot = cur
        for j in range(1, win):
            tot = tot + buf[POOL_HALO - j:POOL_HALO - j + tm, cs]
        cnt = jnp.minimum(pos + 1, win).astype(F32)
        z = tot / cnt - cur
        ys.append((_dot(z.astype(BF16), w_ref[gi]) * sc_ref[:, cs]).astype(BF16))
    y = jnp.concatenate(ys, axis=1)
    if fused:
        o_ref[0] = x_ref[0] + _dot(y, wa_ref[...]) + _dot(odn_ref[0], wb_ref[...])
    else:
        o_ref[0] = y


def _pool(proj, hist, w, scale, tm, p0, fuse=None):
    b, t, _ = proj.shape
    dp = len(POOL_WINDOWS) * LANES
    tm = min(tm, t)
    in_specs = [pl.BlockSpec((1, tm, dp), lambda bi, i: (bi, i, 0)),
                pl.BlockSpec((1, POOL_HALO, dp), lambda bi, i: (bi, 0, 0)),
                _sel(*w), _sel(*scale)]
    args = [proj, hist, w[0], scale[0]]
    out_w, out_dt = dp, BF16
    if fuse is not None:
        o_dn, (w_out, lead), x = fuse
        dd, d = o_dn.shape[2], x.shape[2]
        in_specs += [pl.BlockSpec((1, tm, dd), lambda bi, i: (bi, i, 0)),
                     _sel(w_out, lead, block=(dp, d), index=(0, 0)),
                     _sel(w_out, lead, block=(dd, d), index=(dp // dd, 0)),
                     pl.BlockSpec((1, tm, d), lambda bi, i: (bi, i, 0))]
        args += [o_dn, w_out, w_out, x]
        out_w, out_dt = d, F32
    return pl.pallas_call(
        functools.partial(_pool_kernel, tm=tm, p0=p0, fused=fuse is not None),
        grid=(b, t // tm),
        in_specs=in_specs,
        out_specs=pl.BlockSpec((1, tm, out_w), lambda bi, i: (bi, i, 0)),
        out_shape=jax.ShapeDtypeStruct((b, t, out_w), out_dt),
        scratch_shapes=[pltpu.VMEM((POOL_HALO + tm, dp), F32)],
        compiler_params=_cparams("parallel", "arbitrary"),
        name="pool_mix",
    )(*args)


def _cumsum_rows(x):
    n = x.shape[0]
    row = lax.broadcasted_iota(jnp.int32, x.shape, 0)
    sh = 1
    while sh < n:
        x = x + jnp.where(row >= sh, pltpu.roll(x, sh, axis=0), 0.0)
        sh *= 2
    return x


def _delta_kernel(q_ref, k_ref, v_ref, gate_ref, ba_ref, hist_ref, s0_ref, cw_ref, alog_ref, dtb_ref,
                  nw_ref, o_ref, s_ref, ext, *, rows, nb):
    c = pl.program_id(1)
    C = DN_CHUNK
    dd = DN_HEADS * DN_DIM

    @pl.when(c == 0)
    def _():
        ext[:, 0:DN_HALO, :] = hist_ref[...]
        s_ref[...] = s0_ref[...]

    ri = lax.broadcasted_iota(jnp.int32, (C, C), 0)
    ci = lax.broadcasted_iota(jnp.int32, (C, C), 1)
    incl = ri >= ci
    strict = ri > ci
    eye = (ri == ci).astype(F32)
    valid = lax.broadcasted_iota(jnp.int32, (C, LANES), 0) < rows

    chains = []
    for bi in range(nb):
        ext[bi, DN_HALO:DN_HALO + rows, 0:dd] = q_ref[bi]
        ext[bi, DN_HALO:DN_HALO + rows, dd:2 * dd] = k_ref[bi]
        ext[bi, DN_HALO:DN_HALO + rows, 2 * dd:3 * dd] = v_ref[bi]
        if rows < C:
            ext[bi, DN_HALO + rows:DN_HALO + C, :] = jnp.zeros((C - rows, 3 * dd), F32)
        conv = cw_ref[DN_CONV - 1:DN_CONV, :] * ext[bi, DN_HALO:DN_HALO + C, :]
        for kk in range(DN_CONV - 1):
            off = DN_HALO - (DN_CONV - 1) + kk
            conv = conv + cw_ref[kk:kk + 1, :] * ext[bi, off:off + C, :]
        act = conv * _sigmoid(conv)
        ext[bi, 0:DN_HALO, :] = ext[bi, rows:rows + DN_HALO, :]

        ba = ba_ref[bi]
        gate = gate_ref[bi]
        if rows < C:
            ba = jnp.concatenate([ba, jnp.zeros((C - rows, LANES), F32)], axis=0)
        beta_t = jnp.where(valid, _sigmoid(ba), 0.0)
        xg = ba + dtb_ref[...]
        softplus = jnp.maximum(xg, 0.0) + jnp.log(1.0 + jnp.exp(-jnp.abs(xg)))
        g_t = jnp.where(valid, -jnp.exp(alog_ref[...]) * softplus, 0.0)
        gcum = _cumsum_rows(g_t)
        gcum_t = gcum.T
        e_g = jnp.exp(gcum)
        g_last = gcum[C - 1:C, :]
        e_rev = jnp.exp(g_last - gcum)
        e_last = jnp.exp(g_last)

        for h in range(DN_HEADS):
            hs = slice(h * DN_DIM, (h + 1) * DN_DIM)
            qh = act[:, hs]
            kh = act[:, dd + h * DN_DIM:dd + (h + 1) * DN_DIM]
            vh = act[:, 2 * dd + h * DN_DIM:2 * dd + (h + 1) * DN_DIM]
            qn = qh * lax.rsqrt(jnp.sum(qh * qh, axis=-1, keepdims=True) + EPS) * (DN_DIM ** -0.5)
            kn = kh * lax.rsqrt(jnp.sum(kh * kh, axis=-1, keepdims=True) + EPS)
            beta = beta_t[:, h:h + 1]
            gcol = gcum[:, DN_HEADS + h:DN_HEADS + h + 1]
            grow = gcum_t[DN_HEADS + h:DN_HEADS + h + 1, :]
            eg = e_g[:, DN_HEADS + h:DN_HEADS + h + 1]
            chains.append(dict(
                bi=bi, h=h, hs=hs, qn=qn, kn=kn, kb=kn.astype(BF16), beta=beta,
                decay=jnp.where(incl, jnp.exp(gcol - grow), 0.0),
                rhs_u=(vh * beta).astype(BF16), rhs_w=(kn * (beta * eg)).astype(BF16),
                qg=(qn * eg).astype(BF16), kg=(kn * e_rev[:, DN_HEADS + h:DN_HEADS + h + 1]).astype(BF16),
                el=e_last[:, DN_HEADS + h:DN_HEADS + h + 1], gate=gate[:, hs]))

    for c0 in range(0, len(chains), DN_STAGE_CHAINS):
        grp = chains[c0:c0 + DN_STAGE_CHAINS]
        for ch in grp:
            ch["a"] = jnp.where(strict, _dot_nt(ch["kb"], ch["kb"]) * ch["decay"] * ch["beta"], 0.0)
            a_blk = jnp.where((ri >> 4) == (ci >> 4), ch["a"], 0.0)
            ch["p"] = eye - a_blk
            ch["xp"] = a_blk
        for _ in range(3):
            for ch in grp:
                xb = ch["xp"].astype(BF16)
                ch["xp"] = _dot(xb, xb)
            for ch in grp:
                ch["p"] = ch["p"] + _dot(ch["p"].astype(BF16), ch["xp"].astype(BF16))
        sh = 4
        while (1 << sh) < C and (1 << sh) < rows:
            off_mask = ((ri >> (sh + 1)) == (ci >> (sh + 1))) & ((ri >> sh) != (ci >> sh))
            for ch in grp:
                ch["pb"] = ch["p"].astype(BF16)
                ch["t"] = _dot(ch["pb"], jnp.where(off_mask, ch["a"], 0.0).astype(BF16))
            for ch in grp:
                ch["p"] = ch["p"] - _dot(ch["t"].astype(BF16), ch["pb"])
            sh += 1
        for ch in grp:
            pb = ch["p"].astype(BF16)
            ch["u"] = _dot(pb, ch["rhs_u"])
            ch["w"] = _dot(pb, ch["rhs_w"])
            ch["qk"] = (_dot_nt(ch["qn"].astype(BF16), ch["kb"]) * ch["decay"]).astype(BF16)
        for ch in grp:
            ch["s"] = s_ref[ch["bi"], ch["h"]]
            ch["sb"] = ch["s"].astype(BF16)
            ch["vb"] = (ch["u"] - _dot(ch["w"].astype(BF16), ch["sb"])).astype(BF16)
        for ch in grp:
            ch["o"] = _dot(ch["qg"], ch["sb"]) + _dot(ch["qk"], ch["vb"])
            s_ref[ch["bi"], ch["h"]] = ch["s"] * ch["el"] + _dot_tn(ch["kg"], ch["vb"])
        for ch in grp:
            o = ch["o"]
            o = o * lax.rsqrt(jnp.mean(o * o, axis=-1, keepdims=True) + EPS) * nw_ref[...]
            o = o[0:rows] * (ch["gate"] * _sigmoid(ch["gate"]))
            o_ref[ch["bi"], :, ch["hs"]] = o.astype(o_ref.dtype)


def _delta(proj, hist, s0, conv_w, alog_row, dtb_row, norm_w):
    b, t, _ = proj.shape
    dd = DN_HEADS * DN_DIM
    rows = min(DN_CHUNK, t)
    nb = DN_BATCH
    s_arr, s_lead = s0
    col = lambda j: pl.BlockSpec((nb, rows, dd), lambda bi, c: (bi, c, j))
    st_block = (nb, DN_HEADS, DN_DIM, DN_DIM)
    return pl.pallas_call(
        functools.partial(_delta_kernel, rows=rows, nb=nb),
        grid=(b // nb, t // rows),
        in_specs=[col(1), col(2), col(3), col(4),
                  pl.BlockSpec((nb, rows, LANES), lambda bi, c: (bi, c, 5 * dd // LANES)),
                  pl.BlockSpec((nb, DN_HALO, 3 * dd), lambda bi, c: (bi, 0, 0)),
                  pl.BlockSpec((None,) * len(s_lead) + st_block, lambda bi, c: tuple(s_lead) + (bi, 0, 0, 0)),
                  _sel(*conv_w), _sel(*alog_row), _sel(*dtb_row), _sel(*norm_w)],
        out_specs=[pl.BlockSpec((nb, rows, dd), lambda bi, c: (bi, c, 0)),
                   pl.BlockSpec(st_block, lambda bi, c: (bi, 0, 0, 0))],
        out_shape=[jax.ShapeDtypeStruct((b, t, dd), BF16),
                   jax.ShapeDtypeStruct((b, DN_HEADS, DN_DIM, DN_DIM), F32)],
        scratch_shapes=[pltpu.VMEM((nb, DN_HALO + DN_CHUNK, 3 * dd), F32)],
        compiler_params=_cparams("parallel", "arbitrary"),
        name="delta_rule",
    )(proj, proj, proj, proj, proj, hist, s_arr, conv_w[0], alog_row[0], dtb_row[0], norm_w[0])


def _swa_group_bodies(n, q_ref, k_ref, v_ref, num_s, mx_s, den_s, gi, kst, vst, kpv, vpv, dil, nblk):
    blk = SWA_BLOCK
    span = blk * nblk

    def rows_of(r, first_blk, n_blk):
        if dil == 1:
            return pl.ds(first_blk * blk, n_blk * blk)
        return pl.ds(r + dil * blk * first_blk, n_blk * blk, stride=dil)

    @pl.when(n == 0)
    def _():
        kpv[...] = jnp.zeros_like(kpv)
        vpv[...] = jnp.zeros_like(vpv)

    for r in range(dil):
        kst[r] = k_ref[0, rows_of(r, 0, nblk), :].astype(BF16)
        vst[r] = v_ref[0, rows_of(r, 0, nblk), :].astype(BF16)

    qi = lax.broadcasted_iota(jnp.int32, (blk, blk), 0)
    kj = lax.broadcasted_iota(jnp.int32, (blk, blk), 1)
    below = kj <= qi
    above = kj >= qi
    above_first = above & (n > 0)
    lane = lax.broadcasted_iota(jnp.int32, (1, LANES), 1)
    zero = jnp.zeros((), BF16)
    scale = SWA_DIM ** -0.5 * LOG2_E

    blocks = [(r, j) for r in range(dil) for j in range(nblk)]
    for g0 in range(0, len(blocks), SWA_STAGE_BLOCKS):
        bodies = []
        for r, j in blocks[g0:g0 + SWA_STAGE_BLOCKS]:
            rs = slice(j * blk, (j + 1) * blk)
            q_b = (q_ref[0, rows_of(r, j, 1), :] * scale).astype(BF16)
            if j == 0:
                kp, vp, prev_ok = kpv[r], vpv[r], above_first
            else:
                ps = slice((j - 1) * blk, j * blk)
                kp, vp, prev_ok = kst[r, ps, :], vst[r, ps, :], above
            kc, vc = kst[r, rs, :], vst[r, rs, :]
            for half in range(2):
                hm = (lane < SWA_DIM) if half == 0 else (lane >= SWA_DIM)
                qh = jnp.where(hm, q_b, zero)
                bodies.append(dict(
                    r=r, j=j, half=half, hm=hm, vp=vp, vc=vc,
                    s_p=jnp.where(prev_ok, _dot_nt(qh, kp), -jnp.inf),
                    s_c=jnp.where(below, _dot_nt(qh, kc), -jnp.inf)))
        for bd in bodies:
            bd["m"] = jnp.max(jnp.maximum(bd["s_p"], bd["s_c"]), axis=-1, keepdims=True)
        for bd in bodies:
            e_p = jnp.exp2(bd["s_p"] - bd["m"])
            e_c = jnp.exp2(bd["s_c"] - bd["m"])
            bd["den"] = jnp.sum(e_p + e_c, axis=-1, keepdims=True)
            bd["e_p"], bd["e_c"] = e_p.astype(BF16), e_c.astype(BF16)
        for b0, b1 in zip(bodies[0::2], bodies[1::2]):
            acc = None
            for bd in (b0, b1):
                part = (_dot(bd["e_p"], jnp.where(bd["hm"], bd["vp"], zero))
                        + _dot(bd["e_c"], jnp.where(bd["hm"], bd["vc"], zero)))
                acc = part if acc is None else acc + part
            rows = rows_of(b0["r"], b0["j"], 1)
            num_s[gi, rows, :] = acc
            mx_s[gi, rows, :] = jnp.where(b0["hm"], b0["m"], b1["m"])
            den_s[gi, rows, :] = jnp.where(b0["hm"], b0["den"], b1["den"])

    for r in range(dil):
        kpv[r] = kst[r, span - blk:span, :]
        vpv[r] = vst[r, span - blk:span, :]


def _swa_prompt_kernel(*refs, dils, rows):
    ng = len(dils)
    in_refs, o_ref, scr = refs[:3 * ng], refs[3 * ng], refs[3 * ng + 1:]
    num_s, mx_s, den_s = scr[4 * ng:]
    n = pl.program_id(2)
    for gi, dil in enumerate(dils):
        _swa_group_bodies(n, *in_refs[3 * gi:3 * gi + 3], num_s, mx_s, den_s, gi, *scr[4 * gi:4 * gi + 4],
                          dil, rows // (SWA_BLOCK * dil))
    chunk = 2 * SWA_BLOCK
    for c0 in range(0, rows, chunk):
        rs = slice(c0, c0 + chunk)
        ms = [mx_s[gi, rs, :] for gi in range(ng)]
        mx = functools.reduce(jnp.maximum, ms)
        ws = [jnp.exp2(m - mx) for m in ms]
        tot = sum(w * den_s[gi, rs, :] for gi, w in enumerate(ws))
        o = sum((w / tot) * num_s[gi, rs, :] for gi, w in enumerate(ws))
        o_ref[0, rs, :] = o.astype(o_ref.dtype)


def _swa_prompt(qkv):
    b, s, _ = qkv.shape
    dils = tuple(dil for _, dil in WIN_CONFIGS)
    rows = SWA_BLOCK * max(dils)
    npair = D_SWA // LANES
    in_specs, scratch = [], []
    for gi, dil in enumerate(dils):
        for j in range(3):
            in_specs.append(pl.BlockSpec((1, rows, LANES),
                                         lambda bi, pp, n, gi=gi, j=j: (bi, n, (3 * gi + j) * npair + pp)))
        stage = pltpu.VMEM((dil, rows // dil, LANES), BF16)
        prev = pltpu.VMEM((dil, SWA_BLOCK, LANES), BF16)
        scratch += [stage, stage, prev, prev]
    scratch += [pltpu.VMEM((len(dils), rows, LANES), F32)] * 3
    out = pl.pallas_call(
        functools.partial(_swa_prompt_kernel, dils=dils, rows=rows),
        grid=(b, npair, s // rows),
        in_specs=in_specs,
        out_specs=pl.BlockSpec((1, rows, LANES), lambda bi, pp, n: (bi, n, pp)),
        out_shape=jax.ShapeDtypeStruct((b, s, D_SWA), BF16),
        scratch_shapes=scratch,
        compiler_params=_cparams("parallel", "parallel", "arbitrary"),
        name="swa_prompt",
    )(*([qkv] * (3 * len(dils))))
    return out.reshape(b * s, D_SWA)


def _swa_sample_kernel(qkv_ref, c0_ref, c1_ref, c2_ref, o_ref, *, t_new):
    caches = (c0_ref, c1_ref, c2_ref)
    rows = SWA_HEADS * t_new
    rh = jnp.concatenate([jnp.full((t_new, D_SWA), h, jnp.int32) for h in range(SWA_HEADS)], axis=0)
    chd = lax.broadcasted_iota(jnp.int32, (rows, D_SWA), 1) >> 6
    head_mask = rh == chd
    scale = SWA_DIM ** -0.5
    pad = jnp.zeros((LANES - t_new, D_SWA), BF16)
    pieces = []
    for gi, (win, dil) in enumerate(WIN_CONFIGS):
        c_ref = caches[gi]
        n_hist = c_ref.shape[2]
        base = gi * 3 * D_SWA
        q = qkv_ref[0, :, base:base + D_SWA] * scale
        qrows = jnp.where(head_mask, jnp.concatenate([q] * SWA_HEADS, axis=0), 0.0).astype(BF16)
        tok = jnp.concatenate([lax.broadcasted_iota(jnp.int32, (t_new, n_hist), 0)] * SWA_HEADS, axis=0)
        dist = n_hist + tok - lax.broadcasted_iota(jnp.int32, (rows, n_hist), 1)
        ok = (dist <= win) & ((dist & (dil - 1)) == 0)
        s_c = jnp.where(ok, _dot(qrows, c_ref[0].astype(BF16)), -jnp.inf)
        k_new = jnp.concatenate([qkv_ref[0, :, base + D_SWA:base + 2 * D_SWA].astype(BF16), pad], axis=0)
        v_new = jnp.concatenate([qkv_ref[0, :, base + 2 * D_SWA:base + 3 * D_SWA].astype(BF16), pad], axis=0)
        tok_n = jnp.concatenate([lax.broadcasted_iota(jnp.int32, (t_new, LANES), 0)] * SWA_HEADS, axis=0)
        dn = tok_n - lax.broadcasted_iota(jnp.int32, (rows, LANES), 1)
        ok_n = (dn >= 0) & (dn <= win) & ((dn & (dil - 1)) == 0)
        s_n = jnp.where(ok_n, _dot_nt(qrows, k_new), -jnp.inf)
        pieces.append((s_c, s_n, c_ref, v_new))
    m = None
    for s_c, s_n, _, _ in pieces:
        mm = jnp.maximum(jnp.max(s_c, axis=-1, keepdims=True), jnp.max(s_n, axis=-1, keepdims=True))
        m = mm if m is None else jnp.maximum(m, mm)
    den = jnp.zeros((rows, 1), F32)
    num = jnp.zeros((rows, D_SWA), F32)
    for s_c, s_n, c_ref, v_new in pieces:
        e_c = jnp.exp(s_c - m)
        e_n = jnp.exp(s_n - m)
        den = den + jnp.sum(e_c, axis=-1, keepdims=True) + jnp.sum(e_n, axis=-1, keepdims=True)
        num = num + _dot_nt(e_c.astype(BF16), c_ref[1].astype(BF16)) + _dot(e_n.astype(BF16), v_new)
    o = jnp.where(head_mask, num / den, 0.0)
    out = o[0:t_new]
    for h in range(1, SWA_HEADS):
        out = out + o[h * t_new:(h + 1) * t_new]
    o_ref[0] = out.astype(o_ref.dtype)


def _swa_sample(qkv, caches, layer_idx):
    b, t, c = qkv.shape
    return pl.pallas_call(
        functools.partial(_swa_sample_kernel, t_new=t),
        grid=(b,),
        in_specs=[pl.BlockSpec((1, t, c), lambda bi: (bi, 0, 0))]
        + [pl.BlockSpec((None, None) + ch.shape[2:], lambda bi: (layer_idx, bi, 0, 0, 0)) for ch in caches],
        out_specs=pl.BlockSpec((1, t, D_SWA), lambda bi: (bi, 0, 0)),
        out_shape=jax.ShapeDtypeStruct((b, t, D_SWA), BF16),
        compiler_params=_cparams("parallel"),
        name="swa_sample",
    )(qkv, *caches)


def _mem_attn_kernel(q_ref, k_ref, v_ref, o_ref):
    n, nh, hd = k_ref.shape
    t = q_ref.shape[1]
    q = q_ref[0]
    qs = jnp.concatenate([q[:, h * hd:(h + 1) * hd] for h in range(nh)], axis=0)
    k_all = k_ref[...].reshape(n * nh, hd).astype(BF16)
    v_all = v_ref[...].reshape(n * nh, hd).astype(BF16)
    row_head = jnp.concatenate([jnp.full((t, n * nh), h, jnp.int32) for h in range(nh)], axis=0)
    col_head = lax.broadcasted_iota(jnp.int32, (nh * t, n * nh), 1) & (nh - 1)
    s = jnp.where(row_head == col_head, _dot_nt(qs, k_all) * (hd ** -0.5), -jnp.inf)
    e = jnp.exp(s - jnp.max(s, axis=-1, keepdims=True))
    o = _dot(e.astype(BF16), v_all) / jnp.sum(e, axis=-1, keepdims=True)
    o_ref[0] = jnp.concatenate([o[h * t:(h + 1) * t] for h in range(nh)], axis=1).astype(o_ref.dtype)


def _mem_attn(q, k, v, layer_idx):
    b, t, d = q.shape
    kv_spec = pl.BlockSpec((None, None) + k.shape[2:], lambda bi: (layer_idx, bi, 0, 0, 0))
    return pl.pallas_call(
        _mem_attn_kernel,
        grid=(b,),
        in_specs=[pl.BlockSpec((1, t, d), lambda bi: (bi, 0, 0)), kv_spec, kv_spec],
        out_specs=pl.BlockSpec((1, t, d), lambda bi: (bi, 0, 0)),
        out_shape=jax.ShapeDtypeStruct((b, t, d), BF16),
        compiler_params=_cparams("parallel"),
        name="mem_attn",
    )(q, k, v)


def _mem_block_kernel(x_ref, g_ref, wq_ref, k_ref, v_ref, wo_ref, o_ref):
    x = x_ref[0]
    d = x.shape[1]
    hd = d // MEM_HEADS
    h = _rms(x, g_ref[...]).astype(BF16)
    heads = [slice(i * hd, (i + 1) * hd) for i in range(MEM_HEADS)]
    qs = [(_dot(h, wq_ref[:, sl]) * (hd ** -0.5)).astype(BF16) for sl in heads]
    ss = [_dot_nt(q, k_ref[0, :, sl].astype(BF16)) for q, sl in zip(qs, heads)]
    es = [jnp.exp(s - jnp.max(s, axis=-1, keepdims=True)) for s in ss]
    outs = [(_dot(e.astype(BF16), v_ref[0, :, sl].astype(BF16)) / jnp.sum(e, axis=-1, keepdims=True)).astype(BF16)
            for e, sl in zip(es, heads)]
    o_ref[0] = x + _dot(jnp.concatenate(outs, axis=1), wo_ref[...])


def _mem_block(x, g, wq, k, v, wo, tm):
    b, t, d = x.shape
    n = k.shape[1]
    return pl.pallas_call(
        _mem_block_kernel,
        grid=(b, t // tm),
        in_specs=[pl.BlockSpec((1, tm, d), lambda bi, i: (bi, i, 0)), _sel(*g), _sel(*wq),
                  pl.BlockSpec((1, n, d), lambda bi, i: (bi, 0, 0)),
                  pl.BlockSpec((1, n, d), lambda bi, i: (bi, 0, 0)), _sel(*wo)],
        out_specs=pl.BlockSpec((1, tm, d), lambda bi, i: (bi, i, 0)),
        out_shape=jax.ShapeDtypeStruct((b, t, d), F32),
        compiler_params=_cparams("parallel", "parallel"),
        name="mem_block",
    )(x, g[0], wq[0], k, v, wo[0])


def _ffn_kernel(x_ref, g_ref, wup_ref, cw_ref, wdn_ref, hist_ref, *rest, tm, halo_rows, step, final):
    if final:
        gf_ref, o_ref, st_ref, u_scr, halo, y_scr = rest
    else:
        o_ref, st_ref, u_scr, halo, y_scr = rest
    i = pl.program_id(1)
    dff = wdn_ref.shape[0]
    fc = FFN_CHUNK

    @pl.when(i == 0)
    def _():
        halo[...] = hist_ref[0]

    x = x_ref[0]
    h = _rms(x, g_ref[...]).astype(BF16)
    n_chunks = dff // fc
    o1 = halo_rows - step
    o2 = halo_rows - 2 * step

    def up(c):
        for half in range(2):
            cols = slice(half * dff + c * fc, half * dff + (c + 1) * fc)
            slot = 2 * (c % 2) + half
            u_scr[slot, 0:halo_rows, :] = halo[:, cols]
            u_scr[slot, halo_rows:halo_rows + tm, :] = _dot(h, wup_ref[:, cols])

    up(0)
    for c in range(n_chunks):
        if c + 1 < n_chunks:
            up(c + 1)
        conv = []
        for half in range(2):
            cols = slice(half * dff + c * fc, half * dff + (c + 1) * fc)
            slot = 2 * (c % 2) + half
            conv.append(cw_ref[2:3, cols] * u_scr[slot, halo_rows:halo_rows + tm, :]
                        + cw_ref[1:2, cols] * u_scr[slot, o1:o1 + tm, :]
                        + cw_ref[0:1, cols] * u_scr[slot, o2:o2 + tm, :])
            halo[:, cols] = u_scr[slot, tm:tm + halo_rows, :]
        y_scr[:, c * fc:(c + 1) * fc] = (conv[0] * _sigmoid(conv[0]) * conv[1]).astype(BF16)
    out = x + _dot(y_scr[...], wdn_ref[...])
    if final:
        out = _rms(out, gf_ref[...])
    o_ref[0] = out
    st_ref[0] = halo[...]


def _ffn(x, g, w_up, conv_w, w_down, hist, tm, step, g_final=None):
    b, t, d = x.shape
    dff = w_down[0].shape[-2]
    halo_rows = hist.shape[1]
    tm = min(tm, t)

    def const(arr, lead):
        nl = len(lead)
        shape = tuple(arr.shape[nl:])
        return pl.BlockSpec((None,) * nl + shape, lambda *_: tuple(lead) + (0,) * len(shape),
                            pipeline_mode=pl.Buffered(1))

    in_specs = [pl.BlockSpec((1, tm, d), lambda bi, i: (bi, i, 0)), const(*g), const(*w_up), const(*conv_w),
                const(*w_down), pl.BlockSpec((1, halo_rows, 2 * dff), lambda bi, i: (bi, 0, 0))]
    args = [x, g[0], w_up[0], conv_w[0], w_down[0], hist]
    if g_final is not None:
        in_specs.append(const(g_final, ()))
        args.append(g_final)
    return pl.pallas_call(
        functools.partial(_ffn_kernel, tm=tm, halo_rows=halo_rows, step=step, final=g_final is not None),
        grid=(b, t // tm),
        in_specs=in_specs,
        out_specs=[pl.BlockSpec((1, tm, d), lambda bi, i: (bi, i, 0)),
                   pl.BlockSpec((1, halo_rows, 2 * dff), lambda bi, i: (bi, 0, 0))],
        out_shape=[jax.ShapeDtypeStruct((b, t, d), F32), jax.ShapeDtypeStruct((b, halo_rows, 2 * dff), F32)],
        scratch_shapes=[pltpu.VMEM((4, halo_rows + tm, FFN_CHUNK), F32), pltpu.VMEM((halo_rows, 2 * dff), F32),
                        pltpu.VMEM((tm, dff), BF16)],
        compiler_params=_cparams("parallel", "arbitrary"),
        name="conv_ffn",
    )(*args)


def _pad_rows_top(a, rows):
    return jnp.pad(a, ((0, 0), (rows - a.shape[1], 0), (0, 0)))


def _row3(a):
    return a.reshape(a.shape[0], 1, a.shape[1])


def kernel(x_prompt, x_sample, state_pool, state_dn_conv, state_dn, cache_win_w128, cache_win_w512, cache_win_w2048, cache_mem_k, cache_mem_v, state_ffn_conv, mem_prompt, g_mix, w_in_ab, w_pool, pool_scale, dn_conv_w, dn_a_log, dn_dt_bias, dn_norm_w, w_out_ab, w_qkv_c, w_out_c, g_mem_q, g_mem_kv, w_mem_q, w_mem_k, w_mem_v, w_mem_o, g_ffn, w_up, ffn_conv_w, w_down, g_final):
    bp, s, d = x_prompt.shape
    bs, ts, _ = x_sample.shape
    depth = g_mix.shape[0]
    n_mem = mem_prompt.shape[1]
    dff = w_down.shape[1]
    d_pool = len(POOL_WINDOWS) * LANES
    dd = DN_HEADS * DN_DIM
    hd = d // MEM_HEADS
    win_caches = (cache_win_w128, cache_win_w512, cache_win_w2048)

    n_in = w_in_ab.shape[2]
    n_pad = -(-n_in // LANES) * LANES
    w_in_b = jnp.pad(w_in_ab, ((0, 0), (0, 0), (0, n_pad - n_in))).astype(BF16)
    w_pool_b, w_out_ab_b = w_pool.astype(BF16), w_out_ab.astype(BF16)
    w_qkv_b, w_out_c_b = w_qkv_c.astype(BF16), w_out_c.astype(BF16)
    wq_b, wk_b, wv_b, wo_b = (w.astype(BF16) for w in (w_mem_q, w_mem_k, w_mem_v, w_mem_o))
    w_up_b, w_down_b = w_up.astype(BF16), w_down.astype(BF16)
    g_mix3, g_mem_q3, g_mem_kv3, g_ffn3 = _row3(g_mix), _row3(g_mem_q), _row3(g_mem_kv), _row3(g_ffn)
    pool_scale3, norm_w3 = _row3(pool_scale), _row3(dn_norm_w)
    lane_pad = ((0, 0), (DN_HEADS, LANES - 2 * DN_HEADS))
    alog3, dtb3 = _row3(jnp.pad(dn_a_log, lane_pad)), _row3(jnp.pad(dn_dt_bias, lane_pad))
    g_final2 = g_final.reshape(1, d)

    xp = x_prompt.reshape(bp * s, d)
    xs = x_sample.reshape(bs * ts, d)
    mem_flat = mem_prompt.reshape(bp * n_mem, d)

    pool_p, pool_s, dconv_p, dconv_s, dn_p, dn_s = [], [], [], [], [], []
    win_p = [[] for _ in WIN_CONFIGS]
    win_s = [[] for _ in WIN_CONFIGS]
    memk_p, memv_p, fconv_p, fconv_s = [], [], [], []

    for layer in range(depth):
        if layer % 2 == 0:
            e = layer // 2
            zero_state = jnp.zeros((1, bp, DN_HEADS, DN_DIM, DN_DIM), F32)
            new_x = []
            for x, b, t, hist_pool, hist_conv, s0, p0 in (
                    (xp, bp, s, jnp.zeros((bp, POOL_HALO, d_pool), F32), jnp.zeros((bp, DN_HALO, 3 * dd), F32),
                     (zero_state, (0,)), 0),
                    (xs, bs, ts, _pad_rows_top(state_pool[e], POOL_HALO), _pad_rows_top(state_dn_conv[e], DN_HALO),
                     (state_dn, (e,)), PAST_LEN)):
                proj, = _norm_proj(x, (g_mix3, (layer,)), [(w_in_b, (e,))], [F32], tm=512)
                proj = proj.reshape(b, t, n_pad)
                o_dn, s_new = _delta(proj, hist_conv, s0, (dn_conv_w, (e,)), (alog3, (e,)), (dtb3, (e,)),
                                     (norm_w3, (e,)))
                pool_w = ((w_pool_b, (e,)), (pool_scale3, (e,)))
                if t >= 512:
                    x = _pool(proj, hist_pool, *pool_w, tm=512, p0=p0,
                              fuse=(o_dn, (w_out_ab_b, (e,)), x.reshape(b, t, d))).reshape(b * t, d)
                else:
                    y_pool = _pool(proj, hist_pool, *pool_w, tm=512, p0=p0)
                    x = _proj_res([y_pool.reshape(b * t, d_pool), o_dn.reshape(b * t, dd)],
                                  [(w_out_ab_b, (e,), 0), (w_out_ab_b, (e,), 1)], x, tm=512)
                new_x.append((x, proj, s_new))
            (xp, proj_p, s_p), (xs, proj_s, s_s) = new_x
            pool_p.append(proj_p[:, s - (POOL_HALO - 1):, :d_pool])
            pool_s.append(jnp.concatenate([state_pool[e], proj_s[:, :, :d_pool]], axis=1)[:, -(POOL_HALO - 1):])
            dconv_p.append(proj_p[:, s - (DN_CONV - 1):, d_pool:d_pool + 3 * dd])
            dconv_s.append(jnp.concatenate([state_dn_conv[e], proj_s[:, :, d_pool:d_pool + 3 * dd]],
                                           axis=1)[:, -(DN_CONV - 1):])
            dn_p.append(s_p)
            dn_s.append(s_s)
        else:
            o = layer // 2
            c = w_qkv_c.shape[2]
            qkv_p = _qkv_rope(xp, (g_mix3, (layer,)), (w_qkv_b, (o,)), _rope_tables(jnp.arange(s)), tm=512)
            tab_s = tuple(jnp.tile(u, (bs, 1)) for u in _rope_tables(PAST_LEN + jnp.arange(ts)))
            qkv_s = _qkv_rope(xs, (g_mix3, (layer,)), (w_qkv_b, (o,)), tab_s, tm=256)
            qkv_p3 = qkv_p.reshape(bp, s, c)
            qkv_s3 = qkv_s.reshape(bs, ts, c)
            xp = _proj_res([_swa_prompt(qkv_p3)], [(w_out_c_b, (o,), 0)], xp, tm=512)
            caches = [ch.transpose(0, 1, 3, 4, 5, 2).reshape(ch.shape[:2] + (2, D_SWA, ch.shape[2]))
                      for ch in win_caches]
            o_s = _swa_sample(qkv_s3, caches, o)
            xs = _proj_res([o_s.reshape(bs * ts, D_SWA)], [(w_out_c_b, (o,), 0)], xs, tm=512)
            for gi, (win, _) in enumerate(WIN_CONFIGS):
                base = gi * 3 * D_SWA
                keep = min(win, s)
                kv_p = qkv_p3[:, s - keep:, base + D_SWA:base + 3 * D_SWA]
                win_p[gi].append(kv_p.reshape(bp, keep, 2, SWA_HEADS, SWA_DIM))
                kv_s = qkv_s3[:, :, base + D_SWA:base + 3 * D_SWA]
                win_s[gi].append(kv_s.reshape(bs, ts, 2, SWA_HEADS, SWA_DIM))

        mk, mv = _norm_proj(mem_flat, (g_mem_kv3, (layer,)), [(wk_b, (layer,)), (wv_b, (layer,))], [F32, F32], tm=256)
        memk_p.append(mk.reshape(bp, n_mem, MEM_HEADS, hd))
        memv_p.append(mv.reshape(bp, n_mem, MEM_HEADS, hd))
        xp = _mem_block(xp.reshape(bp, s, d), (g_mem_q3, (layer,)), (wq_b, (layer,)), mk.reshape(bp, n_mem, d),
                        mv.reshape(bp, n_mem, d), (wo_b, (layer,)), tm=512).reshape(bp * s, d)
        q_s, = _norm_proj(xs, (g_mem_q3, (layer,)), [(wq_b, (layer,))], [BF16], tm=512)
        att_s = _mem_attn(q_s.reshape(bs, ts, d), cache_mem_k, cache_mem_v, layer)
        xs = _proj_res([att_s.reshape(bs * ts, d)], [(wo_b, (layer,), 0)], xs, tm=512)

        gf = g_final2 if layer == depth - 1 else None
        ffn_w = ((g_ffn3, (layer,)), (w_up_b, (layer,)), (ffn_conv_w, (layer,)), (w_down_b, (layer,)))
        yp, st_p = _ffn(xp.reshape(bp, s, d), *ffn_w, jnp.zeros((bp, 8, 2 * dff), F32), tm=512, step=1, g_final=gf)
        xp = yp.reshape(bp * s, d)
        fconv_p.append(st_p[:, -2:])
        xs_tm = xs.reshape(bs, ts, d).transpose(1, 0, 2).reshape(1, ts * bs, d)
        hist_tm = state_ffn_conv[layer].transpose(1, 0, 2).reshape(1, 2 * bs, 2 * dff)
        ys, st_s = _ffn(xs_tm, *ffn_w, hist_tm, tm=ts * bs, step=bs, g_final=gf)
        xs = ys.reshape(ts, bs, d).transpose(1, 0, 2).reshape(bs * ts, d)
        fconv_s.append(st_s.reshape(2, bs, 2 * dff).transpose(1, 0, 2))

    return (xp.reshape(bp, s, d), xs.reshape(bs, ts, d),
            jnp.stack(pool_p), jnp.stack(pool_s),
            jnp.stack(dconv_p), jnp.stack(dconv_s),
            jnp.stack(dn_p), jnp.stack(dn_s),
            jnp.stack(win_p[0]), jnp.stack(win_s[0]),
            jnp.stack(win_p[1]), jnp.stack(win_s[1]),
            jnp.stack(win_p[2]), jnp.stack(win_s[2]),
            jnp.stack(memk_p), jnp.stack(memv_p),
            jnp.stack(fconv_p), jnp.stack(fconv_s))
```

```python
import functools

import jax
import jax.numpy as jnp
from jax import lax
from jax.experimental import pallas as pl
from jax.experimental.pallas import tpu as pltpu

F32 = jnp.float32
BF16 = jnp.bfloat16
EPS = 1e-6

PAST_LEN = 8192
POOL_WINDOWS = (2, 4, 8, 16)
POOL_HALO = 16
DN_HEADS = 4
DN_DIM = 128
DN_CONV = 4
DN_CHUNK = 128
DN_HALO = 8
DN_BATCH = 4
DN_STAGE_CHAINS = 16
WIN_CONFIGS = ((128, 1), (512, 4), (2048, 16))
SWA_HEADS = 8
SWA_DIM = 64
D_SWA = SWA_HEADS * SWA_DIM
SWA_BLOCK = 128
SWA_STAGE_BLOCKS = 16
ROT_HALF = 8
ROPE_THETA = 500000.0
LOG2_E = 1.4426950408889634
MEM_HEADS = 4
FFN_CHUNK = 256
LANES = 128
VMEM_LIMIT_BYTES = 56 * 1024 * 1024


def _cparams(*sem):
    return pltpu.CompilerParams(dimension_semantics=sem, vmem_limit_bytes=VMEM_LIMIT_BYTES)


def _rms(x, g):
    return x * lax.rsqrt(jnp.mean(x * x, axis=-1, keepdims=True) + EPS) * g


def _sigmoid(x):
    return 1.0 / (1.0 + jnp.exp(-x))


def _dot(a, b):
    return jnp.dot(a, b, preferred_element_type=F32)


def _dot_nt(a, b):
    return lax.dot_general(a, b, (((1,), (1,)), ((), ())), preferred_element_type=F32)


def _dot_tn(a, b):
    return lax.dot_general(a, b, (((0,), (0,)), ((), ())), preferred_element_type=F32)


def _sel(arr, lead, block=None, index=None):
    nl = len(lead)
    shape = tuple(arr.shape[nl:]) if block is None else tuple(block)
    idx = (0,) * len(shape) if index is None else tuple(index)
    return pl.BlockSpec((None,) * nl + shape, lambda *_: tuple(lead) + idx)


def _norm_proj_kernel(x_ref, g_ref, *refs, n_w, tn):
    w_refs, o_refs = refs[:n_w], refs[n_w:]
    h = _rms(x_ref[...], g_ref[...]).astype(BF16)
    for w_ref, o_ref in zip(w_refs, o_refs):
        n = w_ref.shape[1]
        for j in range(0, n, tn):
            jw = min(tn, n - j)
            o_ref[:, j:j + jw] = _dot(h, w_ref[:, j:j + jw]).astype(o_ref.dtype)


def _norm_proj(x, g, ws, out_dtypes, tm):
    m, d = x.shape
    tm = min(tm, m)
    g_arr, g_lead = g
    in_specs = [pl.BlockSpec((tm, d), lambda i: (i, 0)), _sel(g_arr, g_lead)]
    in_specs += [_sel(w, lead) for w, lead in ws]
    widths = [w.shape[-1] for w, _ in ws]
    return pl.pallas_call(
        functools.partial(_norm_proj_kernel, n_w=len(ws), tn=512),
        grid=(m // tm,),
        in_specs=in_specs,
        out_specs=[pl.BlockSpec((tm, n), lambda i: (i, 0)) for n in widths],
        out_shape=[jax.ShapeDtypeStruct((m, n), dt) for n, dt in zip(widths, out_dtypes)],
        compiler_params=_cparams("parallel"),
        name="norm_proj",
    )(x, g_arr, *[w for w, _ in ws])


def _qkv_rope_kernel(x_ref, g_ref, w_ref, cos_ref, sa_ref, sb_ref, o_ref):
    h = _rms(x_ref[...], g_ref[...]).astype(BF16)
    cos, sa, sb = cos_ref[...], sa_ref[...], sb_ref[...]
    for piece in range(w_ref.shape[1] // D_SWA):
        c0 = piece * D_SWA
        y = _dot(h, w_ref[:, c0:c0 + D_SWA])
        if piece % 3 == 2:
            o_ref[:, c0:c0 + D_SWA] = y
            continue
        for a in range(D_SWA // LANES):
            ya = y[:, a * LANES:(a + 1) * LANES]
            rot = (ya * cos + pltpu.roll(ya, LANES - ROT_HALF, axis=1) * sa
                   + pltpu.roll(ya, ROT_HALF, axis=1) * sb)
            o_ref[:, c0 + a * LANES:c0 + (a + 1) * LANES] = rot


def _rope_tables(pos):
    t = pos.shape[0]
    inv_freq = ROPE_THETA ** (-jnp.arange(ROT_HALF, dtype=F32) / ROT_HALF)
    ang = pos.astype(F32)[:, None] * inv_freq[None, :]
    cos, sin = jnp.cos(ang), jnp.sin(ang)
    rest = SWA_DIM - 2 * ROT_HALF
    c64 = jnp.concatenate([cos, cos, jnp.ones((t, rest), F32)], axis=1)
    a64 = jnp.concatenate([-sin, jnp.zeros((t, SWA_DIM - ROT_HALF), F32)], axis=1)
    b64 = jnp.concatenate([jnp.zeros((t, ROT_HALF), F32), sin, jnp.zeros((t, rest), F32)], axis=1)
    return tuple(jnp.concatenate([u, u], axis=1) for u in (c64, a64, b64))


def _qkv_rope(x, g, w, tables, tm):
    m, d = x.shape
    w_arr, w_lead = w
    g_arr, g_lead = g
    n = w_arr.shape[-1]
    tm = min(tm, m)
    nper = tables[0].shape[0] // tm
    tab_spec = pl.BlockSpec((tm, LANES), lambda i: (i % nper, 0))
    return pl.pallas_call(
        _qkv_rope_kernel,
        grid=(m // tm,),
        in_specs=[pl.BlockSpec((tm, d), lambda i: (i, 0)), _sel(g_arr, g_lead), _sel(w_arr, w_lead),
                  tab_spec, tab_spec, tab_spec],
        out_specs=pl.BlockSpec((tm, n), lambda i: (i, 0)),
        out_shape=jax.ShapeDtypeStruct((m, n), F32),
        compiler_params=_cparams("parallel"),
        name="qkv_rope",
    )(x, g_arr, w_arr, *tables)


def _proj_res_kernel(*refs, n_in):
    a_refs, w_refs = refs[:n_in], refs[n_in:2 * n_in]
    x_ref, o_ref = refs[2 * n_in], refs[2 * n_in + 1]
    acc = x_ref[...]
    for a_ref, w_ref in zip(a_refs, w_refs):
        acc = acc + _dot(a_ref[...], w_ref[...])
    o_ref[...] = acc


def _proj_res(acts, ws, x, tm):
    m, d = x.shape
    tm = min(tm, m)
    in_specs = [pl.BlockSpec((tm, a.shape[1]), lambda i: (i, 0)) for a in acts]
    in_specs += [_sel(w, lead, block=(a.shape[1], d), index=(blk, 0)) for a, (w, lead, blk) in zip(acts, ws)]
    in_specs += [pl.BlockSpec((tm, d), lambda i: (i, 0))]
    return pl.pallas_call(
        functools.partial(_proj_res_kernel, n_in=len(acts)),
        grid=(m // tm,),
        in_specs=in_specs,
        out_specs=pl.BlockSpec((tm, d), lambda i: (i, 0)),
        out_shape=jax.ShapeDtypeStruct((m, d), F32),
        compiler_params=_cparams("parallel"),
        name="proj_res",
    )(*acts, *[w for w, _, _ in ws], x)


def _pool_kernel(a_ref, hist_ref, w_ref, sc_ref, *rest, tm, p0, fused):
    if fused:
        odn_ref, wa_ref, wb_ref, x_ref, o_ref, buf = rest
    else:
        o_ref, buf = rest
    i = pl.program_id(1)

    @pl.when(i == 0)
    def _():
        buf[0:POOL_HALO, :] = hist_ref[0]

    @pl.when(i > 0)
    def _():
        buf[0:POOL_HALO, :] = buf[tm:tm + POOL_HALO, :]

    buf[POOL_HALO:POOL_HALO + tm, :] = a_ref[0]
    pos = p0 + i * tm + lax.broadcasted_iota(jnp.int32, (tm, 1), 0)
    ys = []
    for gi, win in enumerate(POOL_WINDOWS):
        cs = slice(gi * LANES, (gi + 1) * LANES)
        cur = buf[POOL_HALO:POOL_HALO + tm, cs]
        tot = cur
        for j in range(1, win):
            tot = tot + buf[POOL_HALO - j:POOL_HALO - j + tm, cs]
        cnt = jnp.minimum(pos + 1, win).astype(F32)
        z = tot / cnt - cur
        ys.append((_dot(z.astype(BF16), w_ref[gi]) * sc_ref[:, cs]).astype(BF16))
    y = jnp.concatenate(ys, axis=1)
    if fused:
        o_ref[0] = x_ref[0] + _dot(y, wa_ref[...]) + _dot(odn_ref[0], wb_ref[...])
    else:
        o_ref[0] = y


def _pool(proj, hist, w, scale, tm, p0, fuse=None):
    b, t, _ = proj.shape
    dp = len(POOL_WINDOWS) * LANES
    tm = min(tm, t)
    in_specs = [pl.BlockSpec((1, tm, dp), lambda bi, i: (bi, i, 0)),
                pl.BlockSpec((1, POOL_HALO, dp), lambda bi, i: (bi, 0, 0)),
                _sel(*w), _sel(*scale)]
    args = [proj, hist, w[0], scale[0]]
    out_w, out_dt = dp, BF16
    if fuse is not None:
        o_dn, (w_out, lead), x = fuse
        dd, d = o_dn.shape[2], x.shape[2]
        in_specs += [pl.BlockSpec((1, tm, dd), lambda bi, i: (bi, i, 0)),
                     _sel(w_out, lead, block=(dp, d), index=(0, 0)),
                     _sel(w_out, lead, block=(dd, d), index=(dp // dd, 0)),
                     pl.BlockSpec((1, tm, d), lambda bi, i: (bi, i, 0))]
        args += [o_dn, w_out, w_out, x]
        out_w, out_dt = d, F32
    return pl.pallas_call(
        functools.partial(_pool_kernel, tm=tm, p0=p0, fused=fuse is not None),
        grid=(b, t // tm),
        in_specs=in_specs,
        out_specs=pl.BlockSpec((1, tm, out_w), lambda bi, i: (bi, i, 0)),
        out_shape=jax.ShapeDtypeStruct((b, t, out_w), out_dt),
        scratch_shapes=[pltpu.VMEM((POOL_HALO + tm, dp), F32)],
        compiler_params=_cparams("parallel", "arbitrary"),
        name="pool_mix",
    )(*args)


def _cumsum_rows(x):
    n = x.shape[0]
    row = lax.broadcasted_iota(jnp.int32, x.shape, 0)
    sh = 1
    while sh < n:
        x = x + jnp.where(row >= sh, pltpu.roll(x, sh, axis=0), 0.0)
        sh *= 2
    return x


def _delta_kernel(q_ref, k_ref, v_ref, gate_ref, ba_ref, hist_ref, s0_ref, cw_ref, alog_ref, dtb_ref,
                  nw_ref, o_ref, s_ref, ext, *, rows, nb):
    c = pl.program_id(1)
    C = DN_CHUNK
    dd = DN_HEADS * DN_DIM

    @pl.when(c == 0)
    def _():
        ext[:, 0:DN_HALO, :] = hist_ref[...]
        s_ref[...] = s0_ref[...]

    ri = lax.broadcasted_iota(jnp.int32, (C, C), 0)
    ci = lax.broadcasted_iota(jnp.int32, (C, C), 1)
    incl = ri >= ci
    strict = ri > ci
    eye = (ri == ci).astype(F32)
    valid = lax.broadcasted_iota(jnp.int32, (C, LANES), 0) < rows

    chains = []
    for bi in range(nb):
        ext[bi, DN_HALO:DN_HALO + rows, 0:dd] = q_ref[bi]
        ext[bi, DN_HALO:DN_HALO + rows, dd:2 * dd] = k_ref[bi]
        ext[bi, DN_HALO:DN_HALO + rows, 2 * dd:3 * dd] = v_ref[bi]
        if rows < C:
            ext[bi, DN_HALO + rows:DN_HALO + C, :] = jnp.zeros((C - rows, 3 * dd), F32)
        conv = cw_ref[DN_CONV - 1:DN_CONV, :] * ext[bi, DN_HALO:DN_HALO + C, :]
        for kk in range(DN_CONV - 1):
            off = DN_HALO - (DN_CONV - 1) + kk
            conv = conv + cw_ref[kk:kk + 1, :] * ext[bi, off:off + C, :]
        act = conv * _sigmoid(conv)
        ext[bi, 0:DN_HALO, :] = ext[bi, rows:rows + DN_HALO, :]

        ba = ba_ref[bi]
        gate = gate_ref[bi]
        if rows < C:
            ba = jnp.concatenate([ba, jnp.zeros((C - rows, LANES), F32)], axis=0)
        beta_t = jnp.where(valid, _sigmoid(ba), 0.0)
        xg = ba + dtb_ref[...]
        softplus = jnp.maximum(xg, 0.0) + jnp.log(1.0 + jnp.exp(-jnp.abs(xg)))
        g_t = jnp.where(valid, -jnp.exp(alog_ref[...]) * softplus, 0.0)
        gcum = _cumsum_rows(g_t)
        gcum_t = gcum.T
        e_g = jnp.exp(gcum)
        g_last = gcum[C - 1:C, :]
        e_rev = jnp.exp(g_last - gcum)
        e_last = jnp.exp(g_last)

        for h in range(DN_HEADS):
            hs = slice(h * DN_DIM, (h + 1) * DN_DIM)
            qh = act[:, hs]
            kh = act[:, dd + h * DN_DIM:dd + (h + 1) * DN_DIM]
            vh = act[:, 2 * dd + h * DN_DIM:2 * dd + (h + 1) * DN_DIM]
            qn = qh * lax.rsqrt(jnp.sum(qh * qh, axis=-1, keepdims=True) + EPS) * (DN_DIM ** -0.5)
            kn = kh * lax.rsqrt(jnp.sum(kh * kh, axis=-1, keepdims=True) + EPS)
            beta = beta_t[:, h:h + 1]
            gcol = gcum[:, DN_HEADS + h:DN_HEADS + h + 1]
            grow = gcum_t[DN_HEADS + h:DN_HEADS + h + 1, :]
            eg = e_g[:, DN_HEADS + h:DN_HEADS + h + 1]
            chains.append(dict(
                bi=bi, h=h, hs=hs, qn=qn, kn=kn, kb=kn.astype(BF16), beta=beta,
                decay=jnp.where(incl, jnp.exp(gcol - grow), 0.0),
                rhs_u=(vh * beta).astype(BF16), rhs_w=(kn * (beta * eg)).astype(BF16),
                qg=(qn * eg).astype(BF16), kg=(kn * e_rev[:, DN_HEADS + h:DN_HEADS + h + 1]).astype(BF16),
                el=e_last[:, DN_HEADS + h:DN_HEADS + h + 1], gate=gate[:, hs]))

    for c0 in range(0, len(chains), DN_STAGE_CHAINS):
        grp = chains[c0:c0 + DN_STAGE_CHAINS]
        for ch in grp:
            ch["a"] = jnp.where(strict, _dot_nt(ch["kb"], ch["kb"]) * ch["decay"] * ch["beta"], 0.0)
            a_blk = jnp.where((ri >> 4) == (ci >> 4), ch["a"], 0.0)
            ch["p"] = eye - a_blk
            ch["xp"] = a_blk
        for _ in range(3):
            for ch in grp:
                xb = ch["xp"].astype(BF16)
                ch["xp"] = _dot(xb, xb)
            for ch in grp:
                ch["p"] = ch["p"] + _dot(ch["p"].astype(BF16), ch["xp"].astype(BF16))
        sh = 4
        while (1 << sh) < C and (1 << sh) < rows:
            off_mask = ((ri >> (sh + 1)) == (ci >> (sh + 1))) & ((ri >> sh) != (ci >> sh))
            for ch in grp:
                ch["pb"] = ch["p"].astype(BF16)
                ch["t"] = _dot(ch["pb"], jnp.where(off_mask, ch["a"], 0.0).astype(BF16))
            for ch in grp:
                ch["p"] = ch["p"] - _dot(ch["t"].astype(BF16), ch["pb"])
            sh += 1
        for ch in grp:
            pb = ch["p"].astype(BF16)
            ch["u"] = _dot(pb, ch["rhs_u"])
            ch["w"] = _dot(pb, ch["rhs_w"])
            ch["qk"] = (_dot_nt(ch["qn"].astype(BF16), ch["kb"]) * ch["decay"]).astype(BF16)
        for ch in grp:
            ch["s"] = s_ref[ch["bi"], ch["h"]]
            ch["sb"] = ch["s"].astype(BF16)
            ch["vb"] = (ch["u"] - _dot(ch["w"].astype(BF16), ch["sb"])).astype(BF16)
        for ch in grp:
            ch["o"] = _dot(ch["qg"], ch["sb"]) + _dot(ch["qk"], ch["vb"])
            s_ref[ch["bi"], ch["h"]] = ch["s"] * ch["el"] + _dot_tn(ch["kg"], ch["vb"])
        for ch in grp:
            o = ch["o"]
            o = o * lax.rsqrt(jnp.mean(o * o, axis=-1, keepdims=True) + EPS) * nw_ref[...]
            o = o[0:rows] * (ch["gate"] * _sigmoid(ch["gate"]))
            o_ref[ch["bi"], :, ch["hs"]] = o.astype(o_ref.dtype)


def _delta(proj, hist, s0, conv_w, alog_row, dtb_row, norm_w):
    b, t, _ = proj.shape
    dd = DN_HEADS * DN_DIM
    rows = min(DN_CHUNK, t)
    nb = DN_BATCH
    s_arr, s_lead = s0
    col = lambda j: pl.BlockSpec((nb, rows, dd), lambda bi, c: (bi, c, j))
    st_block = (nb, DN_HEADS, DN_DIM, DN_DIM)
    return pl.pallas_call(
        functools.partial(_delta_kernel, rows=rows, nb=nb),
        grid=(b // nb, t // rows),
        in_specs=[col(1), col(2), col(3), col(4),
                  pl.BlockSpec((nb, rows, LANES), lambda bi, c: (bi, c, 5 * dd // LANES)),
                  pl.BlockSpec((nb, DN_HALO, 3 * dd), lambda bi, c: (bi, 0, 0)),
                  pl.BlockSpec((None,) * len(s_lead) + st_block, lambda bi, c: tuple(s_lead) + (bi, 0, 0, 0)),
                  _sel(*conv_w), _sel(*alog_row), _sel(*dtb_row), _sel(*norm_w)],
        out_specs=[pl.BlockSpec((nb, rows, dd), lambda bi, c: (bi, c, 0)),
                   pl.BlockSpec(st_block, lambda bi, c: (bi, 0, 0, 0))],
        out_shape=[jax.ShapeDtypeStruct((b, t, dd), BF16),
                   jax.ShapeDtypeStruct((b, DN_HEADS, DN_DIM, DN_DIM), F32)],
        scratch_shapes=[pltpu.VMEM((nb, DN_HALO + DN_CHUNK, 3 * dd), F32)],
        compiler_params=_cparams("parallel", "arbitrary"),
        name="delta_rule",
    )(proj, proj, proj, proj, proj, hist, s_arr, conv_w[0], alog_row[0], dtb_row[0], norm_w[0])


def _swa_group_bodies(n, q_ref, k_ref, v_ref, num_s, mx_s, den_s, gi, kst, vst, kpv, vpv, dil, nblk):
    blk = SWA_BLOCK
    span = blk * nblk

    def rows_of(r, first_blk, n_blk):
        if dil == 1:
            return pl.ds(first_blk * blk, n_blk * blk)
        return pl.ds(r + dil * blk * first_blk, n_blk * blk, stride=dil)

    @pl.when(n == 0)
    def _():
        kpv[...] = jnp.zeros_like(kpv)
        vpv[...] = jnp.zeros_like(vpv)

    for r in range(dil):
        kst[r] = k_ref[0, rows_of(r, 0, nblk), :].astype(BF16)
        vst[r] = v_ref[0, rows_of(r, 0, nblk), :].astype(BF16)

    qi = lax.broadcasted_iota(jnp.int32, (blk, blk), 0)
    kj = lax.broadcasted_iota(jnp.int32, (blk, blk), 1)
    below = kj <= qi
    above = kj >= qi
    above_first = above & (n > 0)
    lane = lax.broadcasted_iota(jnp.int32, (1, LANES), 1)
    zero = jnp.zeros((), BF16)
    scale = SWA_DIM ** -0.5 * LOG2_E

    blocks = [(r, j) for r in range(dil) for j in range(nblk)]
    for g0 in range(0, len(blocks), SWA_STAGE_BLOCKS):
        bodies = []
        for r, j in blocks[g0:g0 + SWA_STAGE_BLOCKS]:
            rs = slice(j * blk, (j + 1) * blk)
            q_b = (q_ref[0, rows_of(r, j, 1), :] * scale).astype(BF16)
            if j == 0:
                kp, vp, prev_ok = kpv[r], vpv[r], above_first
            else:
                ps = slice((j - 1) * blk, j * blk)
                kp, vp, prev_ok = kst[r, ps, :], vst[r, ps, :], above
            kc, vc = kst[r, rs, :], vst[r, rs, :]
            for half in range(2):
                hm = (lane < SWA_DIM) if half == 0 else (lane >= SWA_DIM)
                qh = jnp.where(hm, q_b, zero)
                bodies.append(dict(
                    r=r, j=j, half=half, hm=hm, vp=vp, vc=vc,
                    s_p=jnp.where(prev_ok, _dot_nt(qh, kp), -jnp.inf),
                    s_c=jnp.where(below, _dot_nt(qh, kc), -jnp.inf)))
        for bd in bodies:
            bd["m"] = jnp.max(jnp.maximum(bd["s_p"], bd["s_c"]), axis=-1, keepdims=True)
        for bd in bodies:
            e_p = jnp.exp2(bd["s_p"] - bd["m"])
            e_c = jnp.exp2(bd["s_c"] - bd["m"])
            bd["den"] = jnp.sum(e_p + e_c, axis=-1, keepdims=True)
            bd["e_p"], bd["e_c"] = e_p.astype(BF16), e_c.astype(BF16)
        for b0, b1 in zip(bodies[0::2], bodies[1::2]):
            acc = None
            for bd in (b0, b1):
                part = (_dot(bd["e_p"], jnp.where(bd["hm"], bd["vp"], zero))
                        + _dot(bd["e_c"], jnp.where(bd["hm"], bd["vc"], zero)))
                acc = part if acc is None else acc + part
            rows = rows_of(b0["r"], b0["j"], 1)
            num_s[gi, rows, :] = acc
            mx_s[gi, rows, :] = jnp.where(b0["hm"], b0["m"], b1["m"])
            den_s[gi, rows, :] = jnp.where(b0["hm"], b0["den"], b1["den"])

    for r in range(dil):
        kpv[r] = kst[r, span - blk:span, :]
        vpv[r] = vst[r, span - blk:span, :]


def _swa_prompt_kernel(*refs, dils, rows):
    ng = len(dils)
    in_refs, o_ref, scr = refs[:3 * ng], refs[3 * ng], refs[3 * ng + 1:]
    num_s, mx_s, den_s = scr[4 * ng:]
    n = pl.program_id(2)
    for gi, dil in enumerate(dils):
        _swa_group_bodies(n, *in_refs[3 * gi:3 * gi + 3], num_s, mx_s, den_s, gi, *scr[4 * gi:4 * gi + 4],
                          dil, rows // (SWA_BLOCK * dil))
    chunk = 2 * SWA_BLOCK
    for c0 in range(0, rows, chunk):
        rs = slice(c0, c0 + chunk)
        ms = [mx_s[gi, rs, :] for gi in range(ng)]
        mx = functools.reduce(jnp.maximum, ms)
        ws = [jnp.exp2(m - mx) for m in ms]
        tot = sum(w * den_s[gi, rs, :] for gi, w in enumerate(ws))
        o = sum((w / tot) * num_s[gi, rs, :] for gi, w in enumerate(ws))
        o_ref[0, rs, :] = o.astype(o_ref.dtype)


def _swa_prompt(qkv):
    b, s, _ = qkv.shape
    dils = tuple(dil for _, dil in WIN_CONFIGS)
    rows = SWA_BLOCK * max(dils)
    npair = D_SWA // LANES
    in_specs, scratch = [], []
    for gi, dil in enumerate(dils):
        for j in range(3):
            in_specs.append(pl.BlockSpec((1, rows, LANES),
                                         lambda bi, pp, n, gi=gi, j=j: (bi, n, (3 * gi + j) * npair + pp)))
        stage = pltpu.VMEM((dil, rows // dil, LANES), BF16)
        prev = pltpu.VMEM((dil, SWA_BLOCK, LANES), BF16)
        scratch += [stage, stage, prev, prev]
    scratch += [pltpu.VMEM((len(dils), rows, LANES), F32)] * 3
    out = pl.pallas_call(
        functools.partial(_swa_prompt_kernel, dils=dils, rows=rows),
        grid=(b, npair, s // rows),
        in_specs=in_specs,
        out_specs=pl.BlockSpec((1, rows, LANES), lambda bi, pp, n: (bi, n, pp)),
        out_shape=jax.ShapeDtypeStruct((b, s, D_SWA), BF16),
        scratch_shapes=scratch,
        compiler_params=_cparams("parallel", "parallel", "arbitrary"),
        name="swa_prompt",
    )(*([qkv] * (3 * len(dils))))
    return out.reshape(b * s, D_SWA)


def _swa_sample_kernel(qkv_ref, c0_ref, c1_ref, c2_ref, o_ref, *, t_new):
    caches = (c0_ref, c1_ref, c2_ref)
    rows = SWA_HEADS * t_new
    rh = jnp.concatenate([jnp.full((t_new, D_SWA), h, jnp.int32) for h in range(SWA_HEADS)], axis=0)
    chd = lax.broadcasted_iota(jnp.int32, (rows, D_SWA), 1) >> 6
    head_mask = rh == chd
    scale = SWA_DIM ** -0.5
    pad = jnp.zeros((LANES - t_new, D_SWA), BF16)
    pieces = []
    for gi, (win, dil) in enumerate(WIN_CONFIGS):
        c_ref = caches[gi]
        n_hist = c_ref.shape[2]
        base = gi * 3 * D_SWA
        q = qkv_ref[0, :, base:base + D_SWA] * scale
        qrows = jnp.where(head_mask, jnp.concatenate([q] * SWA_HEADS, axis=0), 0.0).astype(BF16)
        tok = jnp.concatenate([lax.broadcasted_iota(jnp.int32, (t_new, n_hist), 0)] * SWA_HEADS, axis=0)
        dist = n_hist + tok - lax.broadcasted_iota(jnp.int32, (rows, n_hist), 1)
        ok = (dist <= win) & ((dist & (dil - 1)) == 0)
        s_c = jnp.where(ok, _dot(qrows, c_ref[0].astype(BF16)), -jnp.inf)
        k_new = jnp.concatenate([qkv_ref[0, :, base + D_SWA:base + 2 * D_SWA].astype(BF16), pad], axis=0)
        v_new = jnp.concatenate([qkv_ref[0, :, base + 2 * D_SWA:base + 3 * D_SWA].astype(BF16), pad], axis=0)
        tok_n = jnp.concatenate([lax.broadcasted_iota(jnp.int32, (t_new, LANES), 0)] * SWA_HEADS, axis=0)
        dn = tok_n - lax.broadcasted_iota(jnp.int32, (rows, LANES), 1)
        ok_n = (dn >= 0) & (dn <= win) & ((dn & (dil - 1)) == 0)
        s_n = jnp.where(ok_n, _dot_nt(qrows, k_new), -jnp.inf)
        pieces.append((s_c, s_n, c_ref, v_new))
    m = None
    for s_c, s_n, _, _ in pieces:
        mm = jnp.maximum(jnp.max(s_c, axis=-1, keepdims=True), jnp.max(s_n, axis=-1, keepdims=True))
        m = mm if m is None else jnp.maximum(m, mm)
    den = jnp.zeros((rows, 1), F32)
    num = jnp.zeros((rows, D_SWA), F32)
    for s_c, s_n, c_ref, v_new in pieces:
        e_c = jnp.exp(s_c - m)
        e_n = jnp.exp(s_n - m)
        den = den + jnp.sum(e_c, axis=-1, keepdims=True) + jnp.sum(e_n, axis=-1, keepdims=True)
        num = num + _dot_nt(e_c.astype(BF16), c_ref[1].astype(BF16)) + _dot(e_n.astype(BF16), v_new)
    o = jnp.where(head_mask, num / den, 0.0)
    out = o[0:t_new]
    for h in range(1, SWA_HEADS):
        out = out + o[h * t_new:(h + 1) * t_new]
    o_ref[0] = out.astype(o_ref.dtype)


def _swa_sample(qkv, caches, layer_idx):
    b, t, c = qkv.shape
    return pl.pallas_call(
        functools.partial(_swa_sample_kernel, t_new=t),
        grid=(b,),
        in_specs=[pl.BlockSpec((1, t, c), lambda bi: (bi, 0, 0))]
        + [pl.BlockSpec((None, None) + ch.shape[2:], lambda bi: (layer_idx, bi, 0, 0, 0)) for ch in caches],
        out_specs=pl.BlockSpec((1, t, D_SWA), lambda bi: (bi, 0, 0)),
        out_shape=jax.ShapeDtypeStruct((b, t, D_SWA), BF16),
        compiler_params=_cparams("parallel"),
        name="swa_sample",
    )(qkv, *caches)


def _mem_attn_kernel(q_ref, k_ref, v_ref, o_ref):
    n, nh, hd = k_ref.shape
    t = q_ref.shape[1]
    q = q_ref[0]
    qs = jnp.concatenate([q[:, h * hd:(h + 1) * hd] for h in range(nh)], axis=0)
    k_all = k_ref[...].reshape(n * nh, hd).astype(BF16)
    v_all = v_ref[...].reshape(n * nh, hd).astype(BF16)
    row_head = jnp.concatenate([jnp.full((t, n * nh), h, jnp.int32) for h in range(nh)], axis=0)
    col_head = lax.broadcasted_iota(jnp.int32, (nh * t, n * nh), 1) & (nh - 1)
    s = jnp.where(row_head == col_head, _dot_nt(qs, k_all) * (hd ** -0.5), -jnp.inf)
    e = jnp.exp(s - jnp.max(s, axis=-1, keepdims=True))
    o = _dot(e.astype(BF16), v_all) / jnp.sum(e, axis=-1, keepdims=True)
    o_ref[0] = jnp.concatenate([o[h * t:(h + 1) * t] for h in range(nh)], axis=1).astype(o_ref.dtype)


def _mem_attn(q, k, v, layer_idx):
    b, t, d = q.shape
    kv_spec = pl.BlockSpec((None, None) + k.shape[2:], lambda bi: (layer_idx, bi, 0, 0, 0))
    return pl.pallas_call(
        _mem_attn_kernel,
        grid=(b,),
        in_specs=[pl.BlockSpec((1, t, d), lambda bi: (bi, 0, 0)), kv_spec, kv_spec],
        out_specs=pl.BlockSpec((1, t, d), lambda bi: (bi, 0, 0)),
        out_shape=jax.ShapeDtypeStruct((b, t, d), BF16),
        compiler_params=_cparams("parallel"),
        name="mem_attn",
    )(q, k, v)


def _mem_block_kernel(x_ref, g_ref, wq_ref, k_ref, v_ref, wo_ref, o_ref):
    x = x_ref[0]
    d = x.shape[1]
    hd = d // MEM_HEADS
    h = _rms(x, g_ref[...]).astype(BF16)
    heads = [slice(i * hd, (i + 1) * hd) for i in range(MEM_HEADS)]
    qs = [(_dot(h, wq_ref[:, sl]) * (hd ** -0.5)).astype(BF16) for sl in heads]
    ss = [_dot_nt(q, k_ref[0, :, sl].astype(BF16)) for q, sl in zip(qs, heads)]
    es = [jnp.exp(s - jnp.max(s, axis=-1, keepdims=True)) for s in ss]
    outs = [(_dot(e.astype(BF16), v_ref[0, :, sl].astype(BF16)) / jnp.sum(e, axis=-1, keepdims=True)).astype(BF16)
            for e, sl in zip(es, heads)]
    o_ref[0] = x + _dot(jnp.concatenate(outs, axis=1), wo_ref[...])


def _mem_block(x, g, wq, k, v, wo, tm):
    b, t, d = x.shape
    n = k.shape[1]
    return pl.pallas_call(
        _mem_block_kernel,
        grid=(b, t // tm),
        in_specs=[pl.BlockSpec((1, tm, d), lambda bi, i: (bi, i, 0)), _sel(*g), _sel(*wq),
                  pl.BlockSpec((1, n, d), lambda bi, i: (bi, 0, 0)),
                  pl.BlockSpec((1, n, d), lambda bi, i: (bi, 0, 0)), _sel(*wo)],
        out_specs=pl.BlockSpec((1, tm, d), lambda bi, i: (bi, i, 0)),
        out_shape=jax.ShapeDtypeStruct((b, t, d), F32),
        compiler_params=_cparams("parallel", "parallel"),
        name="mem_block",
    )(x, g[0], wq[0], k, v, wo[0])


def _ffn_kernel(x_ref, g_ref, wup_ref, cw_ref, wdn_ref, hist_ref, *rest, tm, halo_rows, step, final):
    if final:
        gf_ref, o_ref, st_ref, u_scr, halo, y_scr = rest
    else:
        o_ref, st_ref, u_scr, halo, y_scr = rest
    i = pl.program_id(1)
    dff = wdn_ref.shape[0]
    fc = FFN_CHUNK

    @pl.when(i == 0)
    def _():
        halo[...] = hist_ref[0]

    x = x_ref[0]
    h = _rms(x, g_ref[...]).astype(BF16)
    n_chunks = dff // fc
    o1 = halo_rows - step
    o2 = halo_rows - 2 * step

    def up(c):
        for half in range(2):
            cols = slice(half * dff + c * fc, half * dff + (c + 1) * fc)
            slot = 2 * (c % 2) + half
            u_scr[slot, 0:halo_rows, :] = halo[:, cols]
            u_scr[slot, halo_rows:halo_rows + tm, :] = _dot(h, wup_ref[:, cols])

    up(0)
    for c in range(n_chunks):
        if c + 1 < n_chunks:
            up(c + 1)
        conv = []
        for half in range(2):
            cols = slice(half * dff + c * fc, half * dff + (c + 1) * fc)
            slot = 2 * (c % 2) + half
            conv.append(cw_ref[2:3, cols] * u_scr[slot, halo_rows:halo_rows + tm, :]
                        + cw_ref[1:2, cols] * u_scr[slot, o1:o1 + tm, :]
                        + cw_ref[0:1, cols] * u_scr[slot, o2:o2 + tm, :])
            halo[:, cols] = u_scr[slot, tm:tm + halo_rows, :]
        y_scr[:, c * fc:(c + 1) * fc] = (conv[0] * _sigmoid(conv[0]) * conv[1]).astype(BF16)
    out = x + _dot(y_scr[...], wdn_ref[...])
    if final:
        out = _rms(out, gf_ref[...])
    o_ref[0] = out
    st_ref[0] = halo[...]


def _ffn(x, g, w_up, conv_w, w_down, hist, tm, step, g_final=None):
    b, t, d = x.shape
    dff = w_down[0].shape[-2]
    halo_rows = hist.shape[1]
    tm = min(tm, t)

    def const(arr, lead):
        nl = len(lead)
        shape = tuple(arr.shape[nl:])
        return pl.BlockSpec((None,) * nl + shape, lambda *_: tuple(lead) + (0,) * len(shape),
                            pipeline_mode=pl.Buffered(1))

    in_specs = [pl.BlockSpec((1, tm, d), lambda bi, i: (bi, i, 0)), const(*g), const(*w_up), const(*conv_w),
                const(*w_down), pl.BlockSpec((1, halo_rows, 2 * dff), lambda bi, i: (bi, 0, 0))]
    args = [x, g[0], w_up[0], conv_w[0], w_down[0], hist]
    if g_final is not None:
        in_specs.append(const(g_final, ()))
        args.append(g_final)
    return pl.pallas_call(
        functools.partial(_ffn_kernel, tm=tm, halo_rows=halo_rows, step=step, final=g_final is not None),
        grid=(b, t // tm),
        in_specs=in_specs,
        out_specs=[pl.BlockSpec((1, tm, d), lambda bi, i: (bi, i, 0)),
                   pl.BlockSpec((1, halo_rows, 2 * dff), lambda bi, i: (bi, 0, 0))],
        out_shape=[jax.ShapeDtypeStruct((b, t, d), F32), jax.ShapeDtypeStruct((b, halo_rows, 2 * dff), F32)],
        scratch_shapes=[pltpu.VMEM((4, halo_rows + tm, FFN_CHUNK), F32), pltpu.VMEM((halo_rows, 2 * dff), F32),
                        pltpu.VMEM((tm, dff), BF16)],
        compiler_params=_cparams("parallel", "arbitrary"),
        name="conv_ffn",
    )(*args)


def _pad_rows_top(a, rows):
    return jnp.pad(a, ((0, 0), (rows - a.shape[1], 0), (0, 0)))


def _row3(a):
    return a.reshape(a.shape[0], 1, a.shape[1])


def kernel(x_prompt, x_sample, state_pool, state_dn_conv, state_dn, cache_win_w128, cache_win_w512, cache_win_w2048, cache_mem_k, cache_mem_v, state_ffn_conv, mem_prompt, g_mix, w_in_ab, w_pool, pool_scale, dn_conv_w, dn_a_log, dn_dt_bias, dn_norm_w, w_out_ab, w_qkv_c, w_out_c, g_mem_q, g_mem_kv, w_mem_q, w_mem_k, w_mem_v, w_mem_o, g_ffn, w_up, ffn_conv_w, w_down, g_final):
    bp, s, d = x_prompt.shape
    bs, ts, _ = x_sample.shape
    depth = g_mix.shape[0]
    n_mem = mem_prompt.shape[1]
    dff = w_down.shape[1]
    d_pool = len(POOL_WINDOWS) * LANES
    dd = DN_HEADS * DN_DIM
    hd = d // MEM_HEADS
    win_caches = (cache_win_w128, cache_win_w512, cache_win_w2048)

    n_in = w_in_ab.shape[2]
    n_pad = -(-n_in // LANES) * LANES
    w_in_b = jnp.pad(w_in_ab, ((0, 0), (0, 0), (0, n_pad - n_in))).astype(BF16)
    w_pool_b, w_out_ab_b = w_pool.astype(BF16), w_out_ab.astype(BF16)
    w_qkv_b, w_out_c_b = w_qkv_c.astype(BF16), w_out_c.astype(BF16)
    wq_b, wk_b, wv_b, wo_b = (w.astype(BF16) for w in (w_mem_q, w_mem_k, w_mem_v, w_mem_o))
    w_up_b, w_down_b = w_up.astype(BF16), w_down.astype(BF16)
    g_mix3, g_mem_q3, g_mem_kv3, g_ffn3 = _row3(g_mix), _row3(g_mem_q), _row3(g_mem_kv), _row3(g_ffn)
    pool_scale3, norm_w3 = _row3(pool_scale), _row3(dn_norm_w)
    lane_pad = ((0, 0), (DN_HEADS, LANES - 2 * DN_HEADS))
    alog3, dtb3 = _row3(jnp.pad(dn_a_log, lane_pad)), _row3(jnp.pad(dn_dt_bias, lane_pad))
    g_final2 = g_final.reshape(1, d)

    xp = x_prompt.reshape(bp * s, d)
    xs = x_sample.reshape(bs * ts, d)
    mem_flat = mem_prompt.reshape(bp * n_mem, d)

    pool_p, pool_s, dconv_p, dconv_s, dn_p, dn_s = [], [], [], [], [], []
    win_p = [[] for _ in WIN_CONFIGS]
    win_s = [[] for _ in WIN_CONFIGS]
    memk_p, memv_p, fconv_p, fconv_s = [], [], [], []

    for layer in range(depth):
        if layer % 2 == 0:
            e = layer // 2
            zero_state = jnp.zeros((1, bp, DN_HEADS, DN_DIM, DN_DIM), F32)
            new_x = []
            for x, b, t, hist_pool, hist_conv, s0, p0 in (
                    (xp, bp, s, jnp.zeros((bp, POOL_HALO, d_pool), F32), jnp.zeros((bp, DN_HALO, 3 * dd), F32),
                     (zero_state, (0,)), 0),
                    (xs, bs, ts, _pad_rows_top(state_pool[e], POOL_HALO), _pad_rows_top(state_dn_conv[e], DN_HALO),
                     (state_dn, (e,)), PAST_LEN)):
                proj, = _norm_proj(x, (g_mix3, (layer,)), [(w_in_b, (e,))], [F32], tm=512)
                proj = proj.reshape(b, t, n_pad)
                o_dn, s_new = _delta(proj, hist_conv, s0, (dn_conv_w, (e,)), (alog3, (e,)), (dtb3, (e,)),
                                     (norm_w3, (e,)))
                pool_w = ((w_pool_b, (e,)), (pool_scale3, (e,)))
                if t >= 512:
                    x = _pool(proj, hist_pool, *pool_w, tm=512, p0=p0,
                              fuse=(o_dn, (w_out_ab_b, (e,)), x.reshape(b, t, d))).reshape(b * t, d)
                else:
                    y_pool = _pool(proj, hist_pool, *pool_w, tm=512, p0=p0)
                    x = _proj_res([y_pool.reshape(b * t, d_pool), o_dn.reshape(b * t, dd)],
                                  [(w_out_ab_b, (e,), 0), (w_out_ab_b, (e,), 1)], x, tm=512)
                new_x.append((x, proj, s_new))
            (xp, proj_p, s_p), (xs, proj_s, s_s) = new_x
            pool_p.append(proj_p[:, s - (POOL_HALO - 1):, :d_pool])
            pool_s.append(jnp.concatenate([state_pool[e], proj_s[:, :, :d_pool]], axis=1)[:, -(POOL_HALO - 1):])
            dconv_p.append(proj_p[:, s - (DN_CONV - 1):, d_pool:d_pool + 3 * dd])
            dconv_s.append(jnp.concatenate([state_dn_conv[e], proj_s[:, :, d_pool:d_pool + 3 * dd]],
                                           axis=1)[:, -(DN_CONV - 1):])
            dn_p.append(s_p)
            dn_s.append(s_s)
        else:
            o = layer // 2
            c = w_qkv_c.shape[2]
            qkv_p = _qkv_rope(xp, (g_mix3, (layer,)), (w_qkv_b, (o,)), _rope_tables(jnp.arange(s)), tm=512)
            tab_s = tuple(jnp.tile(u, (bs, 1)) for u in _rope_tables(PAST_LEN + jnp.arange(ts)))
            qkv_s = _qkv_rope(xs, (g_mix3, (layer,)), (w_qkv_b, (o,)), tab_s, tm=256)
            qkv_p3 = qkv_p.reshape(bp, s, c)
            qkv_s3 = qkv_s.reshape(bs, ts, c)
            xp = _proj_res([_swa_prompt(qkv_p3)], [(w_out_c_b, (o,), 0)], xp, tm=512)
            caches = [ch.transpose(0, 1, 3, 4, 5, 2).reshape(ch.shape[:2] + (2, D_SWA, ch.shape[2]))
                      for ch in win_caches]
            o_s = _swa_sample(qkv_s3, caches, o)
            xs = _proj_res([o_s.reshape(bs * ts, D_SWA)], [(w_out_c_b, (o,), 0)], xs, tm=512)
            for gi, (win, _) in enumerate(WIN_CONFIGS):
                base = gi * 3 * D_SWA
                keep = min(win, s)
                kv_p = qkv_p3[:, s - keep:, base + D_SWA:base + 3 * D_SWA]
                win_p[gi].append(kv_p.reshape(bp, keep, 2, SWA_HEADS, SWA_DIM))
                kv_s = qkv_s3[:, :, base + D_SWA:base + 3 * D_SWA]
                win_s[gi].append(kv_s.reshape(bs, ts, 2, SWA_HEADS, SWA_DIM))

        mk, mv = _norm_proj(mem_flat, (g_mem_kv3, (layer,)), [(wk_b, (layer,)), (wv_b, (layer,))], [F32, F32], tm=256)
        memk_p.append(mk.reshape(bp, n_mem, MEM_HEADS, hd))
        memv_p.append(mv.reshape(bp, n_mem, MEM_HEADS, hd))
        xp = _mem_block(xp.reshape(bp, s, d), (g_mem_q3, (layer,)), (wq_b, (layer,)), mk.reshape(bp, n_mem, d),
                        mv.reshape(bp, n_mem, d), (wo_b, (layer,)), tm=512).reshape(bp * s, d)
        q_s, = _norm_proj(xs, (g_mem_q3, (layer,)), [(wq_b, (layer,))], [BF16], tm=512)
        att_s = _mem_attn(q_s.reshape(bs, ts, d), cache_mem_k, cache_mem_v, layer)
        xs = _proj_res([att_s.reshape(bs * ts, d)], [(wo_b, (layer,), 0)], xs, tm=512)

        gf = g_final2 if layer == depth - 1 else None
        ffn_w = ((g_ffn3, (layer,)), (w_up_b, (layer,)), (ffn_conv_w, (layer,)), (w_down_b, (layer,)))
        yp, st_p = _ffn(xp.reshape(bp, s, d), *ffn_w, jnp.zeros((bp, 8, 2 * dff), F32), tm=1024, step=1, g_final=gf)
        xp = yp.reshape(bp * s, d)
        fconv_p.append(st_p[:, -2:])
        xs_tm = xs.reshape(bs, ts, d).transpose(1, 0, 2).reshape(1, ts * bs, d)
        hist_tm = state_ffn_conv[layer].transpose(1, 0, 2).reshape(1, 2 * bs, 2 * dff)
        ys, st_s = _ffn(xs_tm, *ffn_w, hist_tm, tm=ts * bs, step=bs, g_final=gf)
        xs = ys.reshape(ts, bs, d).transpose(1, 0, 2).reshape(bs * ts, d)
        fconv_s.append(st_s.reshape(2, bs, 2 * dff).transpose(1, 0, 2))

    return (xp.reshape(bp, s, d), xs.reshape(bs, ts, d),
            jnp.stack(pool_p), jnp.stack(pool_s),
            jnp.stack(dconv_p), jnp.stack(dconv_s),
            jnp.stack(dn_p), jnp.stack(dn_s),
            jnp.stack(win_p[0]), jnp.stack(win_s[0]),
            jnp.stack(win_p[1]), jnp.stack(win_s[1]),
            jnp.stack(win_p[2]), jnp.stack(win_s[2]),
            jnp.stack(memk_p), jnp.stack(memv_p),
            jnp.stack(fconv_p), jnp.stack(fconv_s))
```
